```python
import math
import jax, jax.numpy as jnp
from jax import lax
import numpy as np

D_MODEL = 1024
BATCH = 2
SEQ = 8192
DEPTH = 2

MEM_LEN = 256
GROUP_W = 512
D_MIX = 3 * GROUP_W
DA_HEADS = 4
DA_QK_DIM = GROUP_W // (2 * DA_HEADS)
DA_V_DIM = 2 * DA_QK_DIM
HY_CH = GROUP_W
HY_ORDER = 2
SHORT_K = 3
HF_EMB = 33
HF_BANDS = (HF_EMB - 1) // 2
HF_ORDER = 64
HF_OUT = HY_ORDER * 2 * HY_CH
HF_TARGET = 1e-2
HF_FAST = 0.3
HF_SLOW = 1.5
HF_GAIN = 0.03
MEM_HEADS = 4
MEM_HEAD_DIM = GROUP_W // MEM_HEADS
D_IN = 10 * GROUP_W
ROPE_THETA = 10000.0
Q_BLOCK = 128
EPS = 1e-6

kernel_name = "hymba_style_diffattn_hyena_memxattn"


def rms_norm(x, g):
    x32 = x.astype(jnp.float32)
    y = x32 * lax.rsqrt(jnp.mean(x32 * x32, axis=-1, keepdims=True) + EPS)
    return (y * g.astype(jnp.float32)).astype(x.dtype)


def rope_tables(seq_len, dim):
    pos = jnp.arange(seq_len, dtype=jnp.float32)
    inv_freq = ROPE_THETA ** (-jnp.arange(0, dim, 2, dtype=jnp.float32) / dim)
    ang = pos[:, None] * inv_freq[None, :]
    return jnp.cos(ang)[:, None, None, :], jnp.sin(ang)[:, None, None, :]


def apply_rope(x, cos, sin):
    x32 = x.astype(jnp.float32)
    x1, x2 = jnp.split(x32, 2, axis=-1)
    return jnp.concatenate([x1 * cos - x2 * sin, x2 * cos + x1 * sin], axis=-1).astype(x.dtype)


def diff_attention(q, k, v, lam):
    b, s, h, _, dk = q.shape
    dv = v.shape[-1]
    scale = dk ** -0.5
    kh = k.transpose(0, 2, 1, 3, 4)
    vh = v.transpose(0, 2, 1, 3)
    nb = s // Q_BLOCK
    qb = q.transpose(0, 2, 1, 3, 4).reshape(b, h, nb, Q_BLOCK, 2, dk).transpose(2, 0, 1, 3, 4, 5)

    def block(q_blk):
        scores = jnp.einsum('bhqtd,bhktd->tbhqk', q_blk, kh).astype(jnp.float32) * scale
        p = jax.nn.softmax(scores, axis=-1)
        a = p[0] - lam * p[1]
        return jnp.einsum('bhqk,bhkv->bhqv', a.astype(vh.dtype), vh)

    o = lax.map(block, qb)
    return o.transpose(1, 0, 3, 2, 4).reshape(b, s, h, dv)


def short_conv(u, w, bias):
    c = u.shape[-1]
    y = lax.conv_general_dilated(
        u, w[:, None, :].astype(u.dtype), window_strides=(1,),
        padding=((SHORT_K // 2, SHORT_K // 2),),
        dimension_numbers=('NWC', 'WIO', 'NWC'), feature_group_count=c)
    return y + bias.astype(u.dtype)


def hyena_filters(seq_len, w1, b1, w2, b2, w3, b3, w4, freq):
    t = jnp.linspace(0.0, 1.0, seq_len, dtype=jnp.float32)[:, None]
    w = 2.0 * math.pi * jnp.arange(seq_len, dtype=jnp.float32)[:, None] / seq_len
    f = jnp.linspace(1e-4, HF_BANDS - 1, HF_BANDS, dtype=jnp.float32)[None, :]
    z = jnp.concatenate([t, jnp.cos(f * w), -jnp.sin(f * w)], axis=-1)
    fr = freq.astype(jnp.float32)
    hdn = jnp.sin(fr * (z @ w1.astype(jnp.float32) + b1.astype(jnp.float32)))
    hdn = jnp.sin(fr * (hdn @ w2.astype(jnp.float32) + b2.astype(jnp.float32)))
    hdn = jnp.sin(fr * (hdn @ w3.astype(jnp.float32) + b3.astype(jnp.float32)))
    filt = (hdn @ w4.astype(jnp.float32)).reshape(seq_len, HY_ORDER, 2, HY_CH)
    max_decay = math.log(HF_TARGET) / HF_FAST
    min_decay = math.log(HF_TARGET) / HF_SLOW
    deltas = jnp.abs(jnp.linspace(min_decay, max_decay, HY_CH, dtype=jnp.float32))
    decay = jnp.stack([jnp.exp(-t * deltas), jnp.exp(-t * deltas[::-1])], axis=1)
    return filt * decay[:, None, :, :]


def bidir_fft_conv(u, h_fwd, h_bwd, d_skip):
    seq_len, c = u.shape[1], u.shape[2]
    n_fft = 2 * seq_len
    kc = jnp.concatenate([h_fwd, jnp.zeros((1, c), jnp.float32), h_bwd[:seq_len - 1][::-1]], axis=0)
    k_f = jnp.fft.rfft(kc, n=n_fft, axis=0)
    u32 = u.astype(jnp.float32)
    u_f = jnp.fft.rfft(u32, n=n_fft, axis=1)
    y = jnp.fft.irfft(u_f * k_f[None], n=n_fft, axis=1)[:, :seq_len]
    return (y + u32 * d_skip.astype(jnp.float32)).astype(u.dtype)


def hybrid_layer(x, mem, layer_idx, cos, sin, g_norm, w_in, lq1, lk1, lq2, lk2, g_sub,
                 conv_w, conv_b, hf_w1, hf_b1, hf_w2, hf_b2, hf_w3, hf_b3, hf_w4, hf_freq,
                 hy_skip, g_mem, w_mem_kv, w_out):
    b, s, _ = x.shape
    h = rms_norm(x, g_norm)
    proj = h @ w_in
    gw = GROUP_W
    da_q, da_k, da_v, da_g, hy_u, hy_g, m_q, m_g = jnp.split(
        proj, [gw, 2 * gw, 3 * gw, 4 * gw, 7 * gw, 8 * gw, 9 * gw], axis=-1)

    q = apply_rope(da_q.reshape(b, s, DA_HEADS, 2, DA_QK_DIM), cos, sin)
    k = apply_rope(da_k.reshape(b, s, DA_HEADS, 2, DA_QK_DIM), cos, sin)
    v = da_v.reshape(b, s, DA_HEADS, DA_V_DIM)
    lam_init = 0.8 - 0.6 * math.exp(-0.3 * layer_idx)
    lam = (jnp.exp(jnp.sum(lq1.astype(jnp.float32) * lk1.astype(jnp.float32)))
           - jnp.exp(jnp.sum(lq2.astype(jnp.float32) * lk2.astype(jnp.float32))) + lam_init)
    o = diff_attention(q, k, v, lam)
    o = rms_norm(o, g_sub) * (1.0 - lam_init)
    y_a = o.reshape(b, s, gw) * jax.nn.silu(da_g)

    u = short_conv(hy_u, conv_w, conv_b)
    z, x1, x2 = jnp.split(u, 3, axis=-1)
    filt = hyena_filters(s, hf_w1, hf_b1, hf_w2, hf_b2, hf_w3, hf_b3, hf_w4, hf_freq)
    z = x1 * bidir_fft_conv(z, filt[:, 0, 0], filt[:, 0, 1], hy_skip[0])
    z = x2 * bidir_fft_conv(z, filt[:, 1, 0], filt[:, 1, 1], hy_skip[1])
    y_b = z * jax.nn.silu(hy_g)

    m = rms_norm(mem, g_mem)
    m_k, m_v = jnp.split(m @ w_mem_kv, 2, axis=-1)
    m_k = m_k.reshape(b, MEM_LEN, MEM_HEADS, MEM_HEAD_DIM)
    m_v = m_v.reshape(b, MEM_LEN, MEM_HEADS, MEM_HEAD_DIM)
    qm = m_q.reshape(b, s, MEM_HEADS, MEM_HEAD_DIM)
    sc = jnp.einsum('bshd,bmhd->bhsm', qm, m_k).astype(jnp.float32) * (MEM_HEAD_DIM ** -0.5)
    pm = jax.nn.softmax(sc, axis=-1)
    om = jnp.einsum('bhsm,bmhd->bshd', pm.astype(m_v.dtype), m_v).reshape(b, s, gw)
    y_c = om * jax.nn.silu(m_g)

    y = jnp.concatenate([y_a, y_b, y_c], axis=-1) @ w_out
    return x + y


def setup_inputs(seed: int = 0) -> dict:
    key = jax.random.key(seed)
    ks = jax.random.split(key, 26)
    f32 = jnp.float32

    def nrm(k, shape, scale):
        return jax.random.normal(k, shape, f32) * scale

    return {
        "x": nrm(ks[0], (BATCH, SEQ, D_MODEL), 1.0),
        "mem": nrm(ks[1], (BATCH, MEM_LEN, D_MODEL), 1.0),
        "g_norm": 1.0 + nrm(ks[2], (DEPTH, D_MODEL), 0.01),
        "w_in": nrm(ks[3], (DEPTH, D_MODEL, D_IN), D_MODEL ** -0.5),
        "da_lam_q1": nrm(ks[4], (DEPTH, DA_QK_DIM), 0.1),
        "da_lam_k1": nrm(ks[5], (DEPTH, DA_QK_DIM), 0.1),
        "da_lam_q2": nrm(ks[6], (DEPTH, DA_QK_DIM), 0.1),
        "da_lam_k2": nrm(ks[7], (DEPTH, DA_QK_DIM), 0.1),
        "da_subln_g": 1.0 + nrm(ks[8], (DEPTH, DA_V_DIM), 0.01),
        "hy_conv_w": nrm(ks[9], (DEPTH, SHORT_K, 3 * HY_CH), SHORT_K ** -0.5),
        "hy_conv_b": nrm(ks[10], (DEPTH, 3 * HY_CH), 0.02),
        "hf_w1": nrm(ks[11], (DEPTH, HF_EMB, HF_ORDER), HF_EMB ** -0.5),
        "hf_b1": nrm(ks[12], (DEPTH, HF_ORDER), 0.02),
        "hf_w2": nrm(ks[13], (DEPTH, HF_ORDER, HF_ORDER), HF_ORDER ** -0.5),
        "hf_b2": nrm(ks[14], (DEPTH, HF_ORDER), 0.02),
        "hf_w3": nrm(ks[15], (DEPTH, HF_ORDER, HF_ORDER), HF_ORDER ** -0.5),
        "hf_b3": nrm(ks[16], (DEPTH, HF_ORDER), 0.02),
        "hf_w4": nrm(ks[17], (DEPTH, HF_ORDER, HF_OUT), HF_GAIN * HF_ORDER ** -0.5),
        "hf_freq": 1.0 + nrm(ks[18], (DEPTH, HF_ORDER), 0.01),
        "hy_skip": nrm(ks[19], (DEPTH, HY_ORDER, HY_CH), 0.1),
        "g_mem": 1.0 + nrm(ks[20], (DEPTH, D_MODEL), 0.01),
        "w_mem_kv": nrm(ks[21], (DEPTH, D_MODEL, 2 * GROUP_W), D_MODEL ** -0.5),
        "w_out": nrm(ks[22], (DEPTH, D_MIX, D_MODEL), D_MIX ** -0.5),
        "g_final": 1.0 + nrm(ks[23], (D_MODEL,), 0.01),
    }


def reference(x, mem, g_norm, w_in, da_lam_q1, da_lam_k1, da_lam_q2, da_lam_k2, da_subln_g,
              hy_conv_w, hy_conv_b, hf_w1, hf_b1, hf_w2, hf_b2, hf_w3, hf_b3, hf_w4, hf_freq,
              hy_skip, g_mem, w_mem_kv, w_out, g_final):
    cos, sin = rope_tables(x.shape[1], DA_QK_DIM)
    for l in range(DEPTH):
        x = hybrid_layer(
            x, mem, l, cos, sin, g_norm[l], w_in[l],
            da_lam_q1[l], da_lam_k1[l], da_lam_q2[l], da_lam_k2[l], da_subln_g[l],
            hy_conv_w[l], hy_conv_b[l], hf_w1[l], hf_b1[l], hf_w2[l], hf_b2[l],
            hf_w3[l], hf_b3[l], hf_w4[l], hf_freq[l], hy_skip[l],
            g_mem[l], w_mem_kv[l], w_out[l])
    return rms_norm(x, g_final)
```

```python
import functools
import math

import jax
import jax.numpy as jnp
import numpy as np
from jax import lax
from jax.experimental import pallas as pl
from jax.experimental.pallas import tpu as pltpu

F32 = jnp.float32
BF16 = jnp.bfloat16

GROUP_W = 512
DA_HEADS = 4
DA_QK_DIM = 64
DA_V_DIM = 128
HY_CH = 512
HF_EMB = 33
HF_BANDS = 16
HF_ORDER = 64
HF_TARGET = 1e-2
HF_FAST = 0.3
HF_SLOW = 1.5
MEM_HEADS = 4
MEM_HEAD_DIM = 128
ROPE_THETA = 10000.0
EPS = 1e-6

LANES = 128
VMEM_LIMIT = 56 * 1024 * 1024

FFT_R = 128
FFT_N = FFT_R * FFT_R
SEQ_LEN = FFT_N // 2
FFT_HALF = FFT_R // 2


def _cparams(sem):
    return pltpu.CompilerParams(dimension_semantics=sem, vmem_limit_bytes=VMEM_LIMIT)


def _silu(g):
    return g * (1.0 / (1.0 + jnp.exp(-g)))


def _norm_matmul_kernel(x_ref, g_ref, w_ref, *refs, rope, q_scale):
    if rope:
        cos_ref, sin_ref, o_ref, h_ref = refs
    else:
        o_ref, h_ref = refs
    j = pl.program_id(1)

    @pl.when(j == 0)
    def _():
        x = x_ref[...]
        ms = jnp.mean(x * x, axis=-1, keepdims=True)
        h_ref[...] = (x * lax.rsqrt(ms + EPS) * g_ref[...]).astype(BF16)

    acc = jnp.dot(h_ref[...], w_ref[...], preferred_element_type=F32)
    if rope:
        tm, tn = acc.shape
        scale = jnp.where(j == 0, q_scale, 1.0).astype(F32)
        cos = cos_ref[...]
        sin = sin_ref[...]
        lane = lax.broadcasted_iota(jnp.int32, (tm, LANES), 1)
        first_half = (lane % DA_QK_DIM) < (DA_QK_DIM // 2)
        for c in range(tn // LANES):
            a = acc[:, c * LANES:(c + 1) * LANES]
            partner = jnp.where(first_half,
                                pltpu.roll(a, LANES - DA_QK_DIM // 2, 1),
                                pltpu.roll(a, DA_QK_DIM // 2, 1))
            o_ref[:, c * LANES:(c + 1) * LANES] = ((a * cos + partner * sin) * scale).astype(o_ref.dtype)
    else:
        o_ref[...] = acc.astype(o_ref.dtype)


def _norm_matmul(x, g, w, out_dtype, name, rope_tabs=None, q_scale=1.0, tm=1024, tn=512):
    m, d = x.shape
    n = w.shape[1]
    tm = min(tm, m)
    assert m % tm == 0 and n % tn == 0
    in_specs = [
        pl.BlockSpec((tm, d), lambda i, j: (i, 0)),
        pl.BlockSpec((1, d), lambda i, j: (0, 0)),
        pl.BlockSpec((d, tn), lambda i, j: (0, j)),
    ]
    args = [x, g.reshape(1, d), w]
    if rope_tabs is not None:
        n_pos = rope_tabs[0].shape[0] // tm
        for t in rope_tabs:
            in_specs.append(pl.BlockSpec((tm, LANES), lambda i, j: (i % n_pos, 0)))
            args.append(t)
    return pl.pallas_call(
        functools.partial(_norm_matmul_kernel, rope=rope_tabs is not None, q_scale=q_scale),
        grid=(m // tm, n // tn),
        in_specs=in_specs,
        out_specs=pl.BlockSpec((tm, tn), lambda i, j: (i, j)),
        out_shape=jax.ShapeDtypeStruct((m, n), out_dtype),
        scratch_shapes=[pltpu.VMEM((tm, d), BF16)],
        compiler_params=_cparams(("parallel", "arbitrary")),
        name=name,
    )(*args)


def _diff_attn_kernel(q_ref, k_ref, v_ref, gate_ref, gsub_ref, lq1_ref, lk1_ref, lq2_ref, lk2_ref,
                      o_ref, vt_ref, m_ref, l_ref, acc_ref, *, lam_init, tq, tk, t_chunk):
    seq = k_ref.shape[0]

    @pl.when(pl.program_id(2) == 0)
    def _():
        for c in range(seq // t_chunk):
            blk = v_ref[c * t_chunk:(c + 1) * t_chunk, :].astype(F32)
            vt_ref[:, c * t_chunk:(c + 1) * t_chunk] = blk.T.astype(BF16)

    q = q_ref[...]
    lane = lax.broadcasted_iota(jnp.int32, q.shape, 1)
    zero = jnp.zeros_like(q)
    q2 = jnp.concatenate([jnp.where(lane < DA_QK_DIM, q, zero),
                          jnp.where(lane >= DA_QK_DIM, q, zero)], axis=0)

    m_ref[...] = jnp.full(m_ref.shape, -1e30, F32)
    l_ref[...] = jnp.zeros(l_ref.shape, F32)
    acc_ref[...] = jnp.zeros(acc_ref.shape, F32)

    def body(c, carry):
        off = pl.multiple_of(c * tk, tk)
        kc = k_ref[pl.ds(off, tk), :]
        s = lax.dot_general(kc, q2, (((1,), (1,)), ((), ())), preferred_element_type=F32)
        m_prev = m_ref[...]
        m_new = jnp.maximum(m_prev, jnp.max(s, axis=0, keepdims=True))
        p = jnp.exp(s - m_new)
        alpha = jnp.exp(m_prev - m_new)
        l_ref[...] = alpha * l_ref[...] + jnp.sum(p, axis=0, keepdims=True)
        vtc = vt_ref[:, pl.ds(off, tk)]
        acc_ref[...] = acc_ref[...] * alpha + jnp.dot(vtc, p.astype(BF16), preferred_element_type=F32)
        m_ref[...] = m_new
        return carry

    lax.fori_loop(0, seq // tk, body, 0)

    lam = (jnp.exp(jnp.sum(lq1_ref[...] * lk1_ref[...], axis=-1, keepdims=True))
           - jnp.exp(jnp.sum(lq2_ref[...] * lk2_ref[...], axis=-1, keepdims=True)) + lam_init)
    on = acc_ref[...] / l_ref[...]
    ot = on[:, :tq] - lam * on[:, tq:]
    o = ot.T
    ms = jnp.mean(o * o, axis=-1, keepdims=True)
    y = o * lax.rsqrt(ms + EPS) * gsub_ref[...] * (1.0 - lam_init)
    o_ref[...] = (y * _silu(gate_ref[...])).astype(o_ref.dtype)


def _diff_attention(qk, vm, gates, g_sub, lq1, lk1, lq2, lk2, lam_init, batch, seq, tq=256, tk=512):
    nq = seq // tq
    head_blocks = GROUP_W // LANES
    vec = lambda a: a.reshape(1, -1).astype(F32)
    small = lambda n: pl.BlockSpec((1, n), lambda b, h, i: (0, 0))
    return pl.pallas_call(
        functools.partial(_diff_attn_kernel, lam_init=lam_init, tq=tq, tk=tk, t_chunk=512),
        grid=(batch, DA_HEADS, nq),
        in_specs=[
            pl.BlockSpec((tq, LANES), lambda b, h, i: (b * nq + i, h)),
            pl.BlockSpec((seq, LANES), lambda b, h, i: (b, head_blocks + h)),
            pl.BlockSpec((seq, LANES), lambda b, h, i: (b, h)),
            pl.BlockSpec((tq, LANES), lambda b, h, i: (b * nq + i, h)),
            small(DA_V_DIM), small(DA_QK_DIM), small(DA_QK_DIM), small(DA_QK_DIM), small(DA_QK_DIM),
        ],
        out_specs=pl.BlockSpec((tq, LANES), lambda b, h, i: (b * nq + i, h)),
        out_shape=jax.ShapeDtypeStruct((batch * seq, GROUP_W), BF16),
        scratch_shapes=[
            pltpu.VMEM((LANES, seq), BF16),
            pltpu.VMEM((1, 2 * tq), F32),
            pltpu.VMEM((1, 2 * tq), F32),
            pltpu.VMEM((DA_V_DIM, 2 * tq), F32),
        ],
        compiler_params=_cparams(("parallel", "parallel", "arbitrary")),
        name="diff_attention",
    )(qk, qk, vm, gates, vec(g_sub), vec(lq1), vec(lk1), vec(lq2), vec(lk2))


def _short_conv_kernel(u_ref, w_ref, b_ref, o_ref, *, rows):
    seq = u_ref.shape[0]
    w0 = w_ref[0:1, :]
    w1 = w_ref[1:2, :]
    w2 = w_ref[2:3, :]
    bias = b_ref[...]
    ridx = lax.broadcasted_iota(jnp.int32, (rows, LANES), 0)
    zero_row = jnp.zeros((1, LANES), F32)
    for c in range(seq // rows):
        r0 = c * rows
        x = u_ref[r0:r0 + rows, :]
        before = zero_row if c == 0 else u_ref[r0 - 1:r0, :]
        after = zero_row if r0 + rows == seq else u_ref[r0 + rows:r0 + rows + 1, :]
        prev = jnp.where(ridx == 0, before, pltpu.roll(x, 1, 0))
        nxt = jnp.where(ridx == rows - 1, after, pltpu.roll(x, rows - 1, 0))
        o_ref[r0:r0 + rows, :] = w0 * prev + w1 * x + w2 * nxt + bias


def _short_conv(feats, col_block0, conv_w, conv_b, batch, seq):
    n_cols = conv_w.shape[1]
    return pl.pallas_call(
        functools.partial(_short_conv_kernel, rows=512),
        grid=(batch, n_cols // LANES),
        in_specs=[
            pl.BlockSpec((seq, LANES), lambda b, j: (b, col_block0 + j)),
            pl.BlockSpec((3, LANES), lambda b, j: (0, j)),
            pl.BlockSpec((1, LANES), lambda b, j: (0, j)),
        ],
        out_specs=pl.BlockSpec((seq, LANES), lambda b, j: (b, j)),
        out_shape=jax.ShapeDtypeStruct((batch * seq, n_cols), F32),
        compiler_params=_cparams(("parallel", "parallel")),
        name="short_conv",
    )(feats, conv_w.astype(F32), conv_b.reshape(1, -1).astype(F32))


def _filter_kernel(z_ref, zs_ref, w1_ref, b1_ref, w2_ref, b2_ref, w3_ref, b3_ref, w4f_ref, w4b_ref,
                   fr_ref, df_ref, db_ref, o_ref):
    hp = lax.Precision.HIGHEST
    fr = fr_ref[...]

    def mlp(z):
        h = jnp.sin(fr * (jnp.dot(z, w1_ref[...], precision=hp, preferred_element_type=F32) + b1_ref[...]))
        h = jnp.sin(fr * (jnp.dot(h, w2_ref[...], precision=hp, preferred_element_type=F32) + b2_ref[...]))
        return jnp.sin(fr * (jnp.dot(h, w3_ref[...], precision=hp, preferred_element_type=F32) + b3_ref[...]))

    z = z_ref[...]
    zs = zs_ref[...]
    tl = z.shape[0]
    ff = jnp.dot(mlp(z), w4f_ref[...], precision=hp, preferred_element_type=F32)
    fb = jnp.dot(mlp(zs), w4b_ref[...], precision=hp, preferred_element_type=F32)
    row = pl.program_id(0) * tl + lax.broadcasted_iota(jnp.int32, (tl, 1), 0)
    dec_f = jnp.exp(-z[:, 0:1] * df_ref[...])
    dec_b = jnp.where(row > 0, jnp.exp(-zs[:, 0:1] * db_ref[...]), 0.0)
    c = HY_CH
    o_ref[:, 0 * c:1 * c] = ff[:, :c] * dec_f
    o_ref[:, 1 * c:2 * c] = fb[:, :c] * dec_b
    o_ref[:, 2 * c:3 * c] = ff[:, c:] * dec_f
    o_ref[:, 3 * c:4 * c] = fb[:, c:] * dec_b


def _hyena_filters(z, zs, deltas_f, deltas_b, w1, b1, w2, b2, w3, b3, w4, freq, tl=512):
    seq = z.shape[0]
    k = HF_ORDER
    w1p = jnp.zeros((k, k), F32).at[:HF_EMB].set(w1.astype(F32))
    w4r = w4.astype(F32).reshape(k, 2, 2, HY_CH)
    w4f = w4r[:, :, 0, :].reshape(k, 2 * HY_CH)
    w4b = w4r[:, :, 1, :].reshape(k, 2 * HY_CH)
    row = lambda a: a.reshape(1, -1).astype(F32)
    full = lambda shape: pl.BlockSpec(shape, lambda i: (0, 0))
    return pl.pallas_call(
        _filter_kernel,
        grid=(seq // tl,),
        in_specs=[
            pl.BlockSpec((tl, k), lambda i: (i, 0)), pl.BlockSpec((tl, k), lambda i: (i, 0)),
            full((k, k)), full((1, k)), full((k, k)), full((1, k)), full((k, k)), full((1, k)),
            full((k, 2 * HY_CH)), full((k, 2 * HY_CH)), full((1, k)), full((1, HY_CH)), full((1, HY_CH)),
        ],
        out_specs=pl.BlockSpec((tl, 4 * HY_CH), lambda i: (i, 0)),
        out_shape=jax.ShapeDtypeStruct((seq, 4 * HY_CH), F32),
        compiler_params=_cparams(("parallel",)),
        name="hyena_filters",
    )(z, zs, w1p, row(b1), w2.astype(F32), row(b2), w3.astype(F32), row(b3), w4f, w4b,
      row(freq), deltas_f, deltas_b)


def _dft_tables():
    n2 = np.arange(FFT_R, dtype=np.int64)[:, None, None]
    n1 = np.arange(FFT_HALF, dtype=np.int64)[None, :, None]
    k1 = np.arange(FFT_R, dtype=np.int64)[None, None, :]
    ang = -2.0 * np.pi * ((k1 * (FFT_R * n1 + n2)) % FFT_N) / FFT_N
    gt = np.concatenate([np.cos(ang), np.sin(ang)], axis=-1)
    gtc = np.concatenate([np.cos(ang), -np.sin(ang)], axis=-1)
    k2 = np.arange(FFT_R, dtype=np.int64)[:, None]
    m = np.arange(FFT_R, dtype=np.int64)[None, :]
    a2 = -2.0 * np.pi * ((k2 * m) % FFT_R) / FFT_R
    fr, fi = np.cos(a2), np.sin(a2)
    c2 = np.block([[fr, -fi], [fi, fr]])
    c2c = np.block([[fr, fi], [-fi, fr]])
    return (jnp.asarray(np.stack([gt, gtc]), F32).astype(BF16),
            jnp.asarray(np.stack([c2, c2c]), F32).astype(BF16))


def _dft_stage_a(u_ref, gt_ref, s_ref):
    def body(n2, carry):
        slab = u_ref[pl.ds(n2, FFT_HALF, stride=FFT_R), :].astype(BF16)
        a = lax.dot_general(gt_ref[0, n2], slab, (((0,), (0,)), ((), ())),
                            preferred_element_type=F32)
        s_ref[pl.ds(pl.multiple_of(n2 * 2 * FFT_R, 2 * FFT_R), 2 * FFT_R), :] = a
        return carry
    lax.fori_loop(0, FFT_R, body, 0)


def _gather_k1(s_ref, k1):
    re = s_ref[pl.ds(k1, FFT_R, stride=2 * FFT_R), :]
    im = s_ref[pl.ds(k1 + FFT_R, FFT_R, stride=2 * FFT_R), :]
    return jnp.concatenate([re, im], axis=0).astype(BF16)


def _filter_spectrum_kernel(h_ref, gt_ref, c2_ref, o_ref, s_ref, *, kc):
    kstep = pl.program_id(1)

    @pl.when(kstep == 0)
    def _():
        _dft_stage_a(h_ref, gt_ref, s_ref)

    def body(i, carry):
        a = _gather_k1(s_ref, kstep * kc + i)
        o_ref[0, i] = jnp.dot(c2_ref[0], a, preferred_element_type=F32)
        return carry
    lax.fori_loop(0, kc, body, 0)


def _filter_spectrum(filt, gt_tab, c2_tab, kc=16):
    seq, n_cols = filt.shape
    n_blocks = n_cols // LANES
    per_dir = HY_CH // LANES
    direction = lambda j: (j // per_dir) % 2
    return pl.pallas_call(
        functools.partial(_filter_spectrum_kernel, kc=kc),
        grid=(n_blocks, FFT_R // kc),
        in_specs=[
            pl.BlockSpec((seq, LANES), lambda j, k: (0, j)),
            pl.BlockSpec((1, FFT_R, FFT_HALF, 2 * FFT_R), lambda j, k: (direction(j), 0, 0, 0)),
            pl.BlockSpec((1, 2 * FFT_R, 2 * FFT_R), lambda j, k: (direction(j), 0, 0)),
        ],
        out_specs=pl.BlockSpec((1, kc, 2 * FFT_R, LANES), lambda j, k: (j, k, 0, 0)),
        out_shape=jax.ShapeDtypeStruct((n_blocks, FFT_R, 2 * FFT_R, LANES), F32),
        scratch_shapes=[pltpu.VMEM((FFT_R * 2 * FFT_R, LANES), F32)],
        compiler_params=_cparams(("parallel", "arbitrary")),
        name="filter_spectrum",
    )(filt, gt_tab, c2_tab)


def _hyena_conv_kernel(u_ref, gate_ref, *refs, kc, gated_out):
    if gated_out:
        g2_ref, d_ref, kf_ref, kb_ref, gt_ref, c2_ref, o_ref, s_ref, y_ref = refs
    else:
        d_ref, kf_ref, kb_ref, gt_ref, c2_ref, o_ref, s_ref, y_ref = refs
        g2_ref = None
    kstep = pl.program_id(2)
    seq = u_ref.shape[0]

    @pl.when(kstep == 0)
    def _():
        _dft_stage_a(u_ref, gt_ref, s_ref)

    def body(i, carry):
        k1 = kstep * kc + i
        x = jnp.dot(c2_ref[0], _gather_k1(s_ref, k1), preferred_element_type=F32)
        kk = kf_ref[0, i] + kb_ref[0, i]
        xr, xi = x[:FFT_R], x[FFT_R:]
        kr, ki = kk[:FFT_R], kk[FFT_R:]
        y = jnp.concatenate([xr * kr - xi * ki, xr * ki + xi * kr], axis=0).astype(BF16)
        b = jnp.dot(c2_ref[1], y, preferred_element_type=F32)
        s_ref[pl.ds(k1, FFT_R, stride=2 * FFT_R), :] = b[:FFT_R]
        s_ref[pl.ds(k1 + FFT_R, FFT_R, stride=2 * FFT_R), :] = b[FFT_R:]
        return carry
    lax.fori_loop(0, kc, body, 0)

    @pl.when(kstep == pl.num_programs(2) - 1)
    def _():
        def inv_body(n2, carry):
            b = s_ref[pl.ds(pl.multiple_of(n2 * 2 * FFT_R, 2 * FFT_R), 2 * FFT_R), :].astype(BF16)
            y_ref[pl.ds(n2, FFT_HALF, stride=FFT_R), :] = jnp.dot(gt_ref[0, n2], b,
                                                                  preferred_element_type=F32)
            return carry
        lax.fori_loop(0, FFT_R, inv_body, 0)

        rows = 512
        d = d_ref[...]
        for c in range(seq // rows):
            sl = slice(c * rows, (c + 1) * rows)
            r = gate_ref[sl, :] * (y_ref[sl, :] * (1.0 / FFT_N) + u_ref[sl, :] * d)
            if gated_out:
                r = r * _silu(g2_ref[sl, :])
            o_ref[sl, :] = r.astype(o_ref.dtype)


def _hyena_conv(u_arr, u_blk0, gate_arr, gate_blk0, g2, d_skip, spec, order, gt_tab, c2_tab,
                batch, seq, out_dtype, kc=8):
    assert seq == SEQ_LEN
    per_dir = HY_CH // LANES
    once = pl.Buffered(1)
    col = lambda blk0: pl.BlockSpec((seq, LANES), lambda b, c, k: (b, blk0 + c), pipeline_mode=once)
    in_specs = [col(u_blk0), col(gate_blk0)]
    args = [u_arr, gate_arr]
    if g2 is not None:
        in_specs.append(col(g2[1]))
        args.append(g2[0])
    kspec = lambda base: pl.BlockSpec((1, kc, 2 * FFT_R, LANES), lambda b, c, k: (base + c, k, 0, 0))
    in_specs += [
        pl.BlockSpec((1, LANES), lambda b, c, k: (0, c)),
        kspec(order * 2 * per_dir), kspec(order * 2 * per_dir + per_dir),
        pl.BlockSpec((1, FFT_R, FFT_HALF, 2 * FFT_R), lambda b, c, k: (0, 0, 0, 0), pipeline_mode=once),
        pl.BlockSpec((2, 2 * FFT_R, 2 * FFT_R), lambda b, c, k: (0, 0, 0)),
    ]
    args += [d_skip.reshape(1, -1).astype(F32), spec, spec, gt_tab, c2_tab]
    return pl.pallas_call(
        functools.partial(_hyena_conv_kernel, kc=kc, gated_out=g2 is not None),
        grid=(batch, per_dir, FFT_R // kc),
        in_specs=in_specs,
        out_specs=pl.BlockSpec((seq, LANES), lambda b, c, k: (b, c)),
        out_shape=jax.ShapeDtypeStruct((batch * seq, HY_CH), out_dtype),
        scratch_shapes=[pltpu.VMEM((FFT_R * 2 * FFT_R, LANES), F32), pltpu.VMEM((seq, LANES), F32)],
        compiler_params=_cparams(("parallel", "parallel", "arbitrary")),
        name="hyena_conv_gated" if g2 is not None else "hyena_conv",
    )(*args)


def _out_kernel(ya_ref, yb_ref, mq_ref, mg_ref, mk_ref, mv_ref, w_ref, x_ref, gf_ref, o_ref, *, final):
    gw = GROUP_W
    acc = x_ref[...]
    acc += jnp.dot(ya_ref[...], w_ref[0:gw, :], preferred_element_type=F32)
    acc += jnp.dot(yb_ref[...], w_ref[gw:2 * gw, :], preferred_element_type=F32)
    hd = MEM_HEAD_DIM
    for h in range(MEM_HEADS):
        sl = slice(h * hd, (h + 1) * hd)
        s = lax.dot_general(mq_ref[:, sl], mk_ref[:, sl], (((1,), (1,)), ((), ())),
                            preferred_element_type=F32) * (hd ** -0.5)
        e = jnp.exp(s - jnp.max(s, axis=-1, keepdims=True))
        p = e / jnp.sum(e, axis=-1, keepdims=True)
        oh = jnp.dot(p.astype(BF16), mv_ref[:, sl], preferred_element_type=F32)
        yc = (oh * _silu(mg_ref[:, sl])).astype(BF16)
        acc += jnp.dot(yc, w_ref[2 * gw + h * hd:2 * gw + (h + 1) * hd, :], preferred_element_type=F32)
    if final:
        ms = jnp.mean(acc * acc, axis=-1, keepdims=True)
        acc = acc * lax.rsqrt(ms + EPS) * gf_ref[...]
    o_ref[...] = acc


def _out_proj(ya, yb, vm, feats, mg_blk, mkv, w_out, x, g_final, final, seq, mem_len, tm=512):
    m, d = x.shape
    gw = GROUP_W
    per_batch = seq // tm
    return pl.pallas_call(
        functools.partial(_out_kernel, final=final),
        grid=(m // tm,),
        in_specs=[
            pl.BlockSpec((tm, gw), lambda i: (i, 0)),
            pl.BlockSpec((tm, gw), lambda i: (i, 0)),
            pl.BlockSpec((tm, gw), lambda i: (i, 1)),
            pl.BlockSpec((tm, gw), lambda i: (i, mg_blk)),
            pl.BlockSpec((mem_len, gw), lambda i: (i // per_batch, 0)),
            pl.BlockSpec((mem_len, gw), lambda i: (i // per_batch, 1)),
            pl.BlockSpec((3 * gw, d), lambda i: (0, 0)),
            pl.BlockSpec((tm, d), lambda i: (i, 0)),
            pl.BlockSpec((1, d), lambda i: (0, 0)),
        ],
        out_specs=pl.BlockSpec((tm, d), lambda i: (i, 0)),
        out_shape=jax.ShapeDtypeStruct((m, d), F32),
        compiler_params=_cparams(("parallel",)),
        name="out_proj_final" if final else "out_proj",
    )(ya, yb, vm, feats, mkv, mkv, w_out, x, g_final.reshape(1, d).astype(F32))


def _rope_tables(seq):
    half = DA_QK_DIM // 2
    pos = jnp.arange(seq, dtype=F32)
    inv_freq = ROPE_THETA ** (-jnp.arange(0, DA_QK_DIM, 2, dtype=F32) / DA_QK_DIM)
    ang = pos[:, None] * inv_freq[None, :]
    cos, sin = jnp.cos(ang), jnp.sin(ang)
    reps = LANES // DA_QK_DIM
    return (jnp.tile(jnp.concatenate([cos, cos], axis=-1), (1, reps)),
            jnp.tile(jnp.concatenate([-sin, sin], axis=-1), (1, reps)))


def _filter_features(seq):
    t = jnp.linspace(0.0, 1.0, seq, dtype=F32)[:, None]
    w = 2.0 * math.pi * jnp.arange(seq, dtype=F32)[:, None] / seq
    f = jnp.linspace(1e-4, HF_BANDS - 1, HF_BANDS, dtype=F32)[None, :]
    z = jnp.concatenate([t, jnp.cos(f * w), -jnp.sin(f * w)], axis=-1)
    z = jnp.pad(z, ((0, 0), (0, HF_ORDER - HF_EMB)))
    zs = jnp.concatenate([z[:1], z[:-1]], axis=0)
    max_decay = math.log(HF_TARGET) / HF_FAST
    min_decay = math.log(HF_TARGET) / HF_SLOW
    deltas = jnp.abs(jnp.linspace(min_decay, max_decay, HY_CH, dtype=F32))
    return z, zs, deltas[None, :], deltas[::-1][None, :]


def kernel(x, mem, g_norm, w_in, da_lam_q1, da_lam_k1, da_lam_q2, da_lam_k2, da_subln_g, hy_conv_w, hy_conv_b, hf_w1, hf_b1, hf_w2, hf_b2, hf_w3, hf_b3, hf_w4, hf_freq, hy_skip, g_mem, w_mem_kv, w_out, g_final):
    batch, seq, d = x.shape
    mem_len = mem.shape[1]
    depth = w_in.shape[0]
    gw = GROUP_W
    xf = x.reshape(batch * seq, d).astype(F32)
    memf = mem.reshape(batch * mem_len, d).astype(F32)
    rope_tabs = _rope_tables(seq)
    z, zs, deltas_f, deltas_b = _filter_features(seq)
    gt_tab, c2_tab = _dft_tables()

    for l in range(depth):
        w = w_in[l].astype(BF16)
        w_qk = w[:, :2 * gw]
        w_vm = jnp.concatenate([w[:, 2 * gw:3 * gw], w[:, 8 * gw:9 * gw]], axis=1)
        w_ft = jnp.concatenate([w[:, 3 * gw:8 * gw], w[:, 9 * gw:]], axis=1)

        qk = _norm_matmul(xf, g_norm[l], w_qk, BF16, "proj_qk_rope", rope_tabs=rope_tabs,
                          q_scale=DA_QK_DIM ** -0.5)
        vm = _norm_matmul(xf, g_norm[l], w_vm, BF16, "proj_v_memq")
        feats = _norm_matmul(xf, g_norm[l], w_ft, F32, "proj_features")
        mkv = _norm_matmul(memf, g_mem[l], w_mem_kv[l].astype(BF16), BF16, "proj_mem_kv")

        lam_init = 0.8 - 0.6 * math.exp(-0.3 * l)
        ya = _diff_attention(qk, vm, feats, da_subln_g[l], da_lam_q1[l], da_lam_k1[l],
                             da_lam_q2[l], da_lam_k2[l], lam_init, batch, seq)

        blk = gw // LANES
        u = _short_conv(feats, blk, hy_conv_w[l], hy_conv_b[l], batch, seq)
        filt = _hyena_filters(z, zs, deltas_f, deltas_b, hf_w1[l], hf_b1[l], hf_w2[l], hf_b2[l],
                              hf_w3[l], hf_b3[l], hf_w4[l], hf_freq[l])
        spec = _filter_spectrum(filt, gt_tab, c2_tab)
        z1 = _hyena_conv(u, 0, u, blk, None, hy_skip[l, 0], spec, 0, gt_tab, c2_tab, batch, seq, F32)
        yb = _hyena_conv(z1, 0, u, 2 * blk, (feats, 4 * blk), hy_skip[l, 1], spec, 1, gt_tab, c2_tab,
                         batch, seq, BF16)

        xf = _out_proj(ya, yb, vm, feats, 5, mkv, w_out[l].astype(BF16), xf, g_final,
                       l == depth - 1, seq, mem_len)
    return xf.reshape(batch, seq, d)
```

```python
import functools
import math

import jax
import jax.numpy as jnp
import numpy as np
from jax import lax
from jax.experimental import pallas as pl
from jax.experimental.pallas import tpu as pltpu

F32 = jnp.float32
BF16 = jnp.bfloat16

GROUP_W = 512
DA_HEADS = 4
DA_QK_DIM = 64
DA_V_DIM = 128
HY_CH = 512
HF_EMB = 33
HF_BANDS = 16
HF_ORDER = 64
HF_TARGET = 1e-2
HF_FAST = 0.3
HF_SLOW = 1.5
MEM_HEADS = 4
MEM_HEAD_DIM = 128
ROPE_THETA = 10000.0
EPS = 1e-6

LANES = 128
VMEM_LIMIT = 56 * 1024 * 1024

FFT_R = 128
FFT_N = FFT_R * FFT_R
SEQ_LEN = FFT_N // 2
FFT_HALF = FFT_R // 2
FFT_UNROLL = 4
FFT_GROUP = 4


def _cparams(sem):
    return pltpu.CompilerParams(dimension_semantics=sem, vmem_limit_bytes=VMEM_LIMIT)


def _silu(g):
    return g * (1.0 / (1.0 + jnp.exp(-g)))


def _norm_matmul_kernel(x_ref, g_ref, w_ref, *refs, rope, q_scale):
    if rope:
        cos_ref, sin_ref, o_ref, h_ref = refs
    else:
        o_ref, h_ref = refs
    j = pl.program_id(1)

    @pl.when(j == 0)
    def _():
        x = x_ref[...]
        ms = jnp.mean(x * x, axis=-1, keepdims=True)
        h_ref[...] = (x * lax.rsqrt(ms + EPS) * g_ref[...]).astype(BF16)

    acc = jnp.dot(h_ref[...], w_ref[...], preferred_element_type=F32)
    if rope:
        tm, tn = acc.shape
        scale = jnp.where(j == 0, q_scale, 1.0).astype(F32)
        cos = cos_ref[...]
        sin = sin_ref[...]
        lane = lax.broadcasted_iota(jnp.int32, (tm, LANES), 1)
        first_half = (lane % DA_QK_DIM) < (DA_QK_DIM // 2)
        for c in range(tn // LANES):
            a = acc[:, c * LANES:(c + 1) * LANES]
            partner = jnp.where(first_half,
                                pltpu.roll(a, LANES - DA_QK_DIM // 2, 1),
                                pltpu.roll(a, DA_QK_DIM // 2, 1))
            o_ref[:, c * LANES:(c + 1) * LANES] = ((a * cos + partner * sin) * scale).astype(o_ref.dtype)
    else:
        o_ref[...] = acc.astype(o_ref.dtype)


def _norm_matmul(x, g, w, out_dtype, name, rope_tabs=None, q_scale=1.0, tm=1024, tn=512):
    m, d = x.shape
    n = w.shape[1]
    tm = min(tm, m)
    assert m % tm == 0 and n % tn == 0
    in_specs = [
        pl.BlockSpec((tm, d), lambda i, j: (i, 0)),
        pl.BlockSpec((1, d), lambda i, j: (0, 0)),
        pl.BlockSpec((d, tn), lambda i, j: (0, j)),
    ]
    args = [x, g.reshape(1, d), w]
    if rope_tabs is not None:
        n_pos = rope_tabs[0].shape[0] // tm
        for t in rope_tabs:
            in_specs.append(pl.BlockSpec((tm, LANES), lambda i, j: (i % n_pos, 0)))
            args.append(t)
    return pl.pallas_call(
        functools.partial(_norm_matmul_kernel, rope=rope_tabs is not None, q_scale=q_scale),
        grid=(m // tm, n // tn),
        in_specs=in_specs,
        out_specs=pl.BlockSpec((tm, tn), lambda i, j: (i, j)),
        out_shape=jax.ShapeDtypeStruct((m, n), out_dtype),
        scratch_shapes=[pltpu.VMEM((tm, d), BF16)],
        compiler_params=_cparams(("parallel", "arbitrary")),
        name=name,
    )(*args)


def _diff_attn_kernel(q_ref, k_ref, v_ref, gate_ref, gsub_ref, lq1_ref, lk1_ref, lq2_ref, lk2_ref,
                      o_ref, vt_ref, m_ref, l_ref, acc_ref, s_a, s_b, p_a, p_b, *, lam_init, tq, tk, t_chunk):
    seq = k_ref.shape[0]
    assert (seq // tk) % 2 == 0 and seq // tk >= 2

    @pl.when(pl.program_id(2) == 0)
    def _():
        for c in range(seq // t_chunk):
            blk = v_ref[c * t_chunk:(c + 1) * t_chunk, :].astype(F32)
            vt_ref[:, c * t_chunk:(c + 1) * t_chunk] = blk.T.astype(BF16)

    q = q_ref[...]
    lane = lax.broadcasted_iota(jnp.int32, q.shape, 1)
    zero = jnp.zeros_like(q)
    q2 = jnp.concatenate([jnp.where(lane < DA_QK_DIM, q, zero),
                          jnp.where(lane >= DA_QK_DIM, q, zero)], axis=0)

    m_ref[...] = jnp.full(m_ref.shape, -1e30, F32)
    l_ref[...] = jnp.zeros(l_ref.shape, F32)
    acc_ref[...] = jnp.zeros(acc_ref.shape, F32)

    def scores(c):
        kc = k_ref[pl.ds(pl.multiple_of(c * tk, tk), tk), :]
        return lax.dot_general(kc, q2, (((1,), (1,)), ((), ())), preferred_element_type=F32)

    def weighted_values(c, p_ref):
        vtc = vt_ref[:, pl.ds(pl.multiple_of(c * tk, tk), tk)]
        return jnp.dot(vtc, p_ref[...], preferred_element_type=F32)

    def softmax_chunk(s_ref, p_ref):
        s = s_ref[...]
        m_prev = m_ref[...]
        m_new = jnp.maximum(m_prev, jnp.max(s, axis=0, keepdims=True))
        p = jnp.exp2(s - m_new)
        alpha = jnp.exp2(m_prev - m_new)
        l_ref[...] = alpha * l_ref[...] + jnp.sum(p, axis=0, keepdims=True)
        m_ref[...] = m_new
        p_ref[...] = p.astype(BF16)
        return alpha

    def step(c, s_cur, s_next, p_cur, p_prev):
        s_next[...] = scores(c + 1)
        pending = weighted_values(c - 1, p_prev)
        alpha = softmax_chunk(s_cur, p_cur)
        acc_ref[...] = (acc_ref[...] + pending) * alpha

    n_chunks = seq // tk
    s_a[...] = scores(0)
    s_b[...] = scores(1)
    softmax_chunk(s_a, p_a)

    def pair(j, carry):
        c = 1 + 2 * j
        step(c, s_b, s_a, p_b, p_a)
        step(c + 1, s_a, s_b, p_a, p_b)
        return carry
    lax.fori_loop(0, (n_chunks - 2) // 2, pair, 0)

    last = n_chunks - 1
    pending = weighted_values(last - 1, p_a)
    alpha = softmax_chunk(s_b, p_b)
    acc_ref[...] = (acc_ref[...] + pending) * alpha + weighted_values(last, p_b)

    lam =(jnp.exp(jnp.sum(lq1_ref[...] * lk1_ref[...], axis=-1, keepdims=True))
           - jnp.exp(jnp.sum(lq2_ref[...] * lk2_ref[...], axis=-1, keepdims=True)) + lam_init)
    on = acc_ref[...] / l_ref[...]
    ot = on[:, :tq] - lam * on[:, tq:]
    o = ot.T
    ms = jnp.mean(o * o, axis=-1, keepdims=True)
    y = o * lax.rsqrt(ms + EPS) * gsub_ref[...] * (1.0 - lam_init)
    o_ref[...] = (y * _silu(gate_ref[...])).astype(o_ref.dtype)


def _diff_attention(qk, vm, gates, g_sub, lq1, lk1, lq2, lk2, lam_init, batch, seq, tq=512, tk=512):
    nq = seq // tq
    head_blocks = GROUP_W // LANES
    vec = lambda a: a.reshape(1, -1).astype(F32)
    small = lambda n: pl.BlockSpec((1, n), lambda b, h, i: (0, 0))
    return pl.pallas_call(
        functools.partial(_diff_attn_kernel, lam_init=lam_init, tq=tq, tk=tk, t_chunk=512),
        grid=(batch, DA_HEADS, nq),
        in_specs=[
            pl.BlockSpec((tq, LANES), lambda b, h, i: (b * nq + i, h)),
            pl.BlockSpec((seq, LANES), lambda b, h, i: (b, head_blocks + h)),
            pl.BlockSpec((seq, LANES), lambda b, h, i: (b, h)),
            pl.BlockSpec((tq, LANES), lambda b, h, i: (b * nq + i, h)),
            small(DA_V_DIM), small(DA_QK_DIM), small(DA_QK_DIM), small(DA_QK_DIM), small(DA_QK_DIM),
        ],
        out_specs=pl.BlockSpec((tq, LANES), lambda b, h, i: (b * nq + i, h)),
        out_shape=jax.ShapeDtypeStruct((batch * seq, GROUP_W), BF16),
        scratch_shapes=[
            pltpu.VMEM((LANES, seq), BF16),
            pltpu.VMEM((1, 2 * tq), F32),
            pltpu.VMEM((1, 2 * tq), F32),
            pltpu.VMEM((DA_V_DIM, 2 * tq), F32),
            pltpu.VMEM((tk, 2 * tq), F32), pltpu.VMEM((tk, 2 * tq), F32),
            pltpu.VMEM((tk, 2 * tq), BF16), pltpu.VMEM((tk, 2 * tq), BF16),
        ],
        compiler_params=_cparams(("parallel", "parallel", "arbitrary")),
        name="diff_attention",
    )(qk, qk, vm, gates, vec(g_sub), vec(lq1), vec(lk1), vec(lq2), vec(lk2))


def _short_conv_kernel(u_ref, w_ref, b_ref, o_ref, *, rows):
    seq = u_ref.shape[0]
    w0 = w_ref[0:1, :]
    w1 = w_ref[1:2, :]
    w2 = w_ref[2:3, :]
    bias = b_ref[...]
    ridx = lax.broadcasted_iota(jnp.int32, (rows, LANES), 0)
    zero_row = jnp.zeros((1, LANES), F32)
    for c in range(seq // rows):
        r0 = c * rows
        x = u_ref[r0:r0 + rows, :]
        before = zero_row if c == 0 else u_ref[r0 - 1:r0, :]
        after = zero_row if r0 + rows == seq else u_ref[r0 + rows:r0 + rows + 1, :]
        prev = jnp.where(ridx == 0, before, pltpu.roll(x, 1, 0))
        nxt = jnp.where(ridx == rows - 1, after, pltpu.roll(x, rows - 1, 0))
        o_ref[r0:r0 + rows, :] = w0 * prev + w1 * x + w2 * nxt + bias


def _short_conv(feats, col_block0, conv_w, conv_b, batch, seq):
    n_cols = conv_w.shape[1]
    return pl.pallas_call(
        functools.partial(_short_conv_kernel, rows=512),
        grid=(batch, n_cols // LANES),
        in_specs=[
            pl.BlockSpec((seq, LANES), lambda b, j: (b, col_block0 + j)),
            pl.BlockSpec((3, LANES), lambda b, j: (0, j)),
            pl.BlockSpec((1, LANES), lambda b, j: (0, j)),
        ],
        out_specs=pl.BlockSpec((seq, LANES), lambda b, j: (b, j)),
        out_shape=jax.ShapeDtypeStruct((batch * seq, n_cols), F32),
        compiler_params=_cparams(("parallel", "parallel")),
        name="short_conv",
    )(feats, conv_w.astype(F32), conv_b.reshape(1, -1).astype(F32))


def _filter_kernel(z_ref, zs_ref, w1_ref, b1_ref, w2_ref, b2_ref, w3_ref, b3_ref, w4f_ref, w4b_ref,
                   fr_ref, df_ref, db_ref, o_ref):
    hp = lax.Precision.HIGHEST
    fr = fr_ref[...]

    def mlp(z):
        h = jnp.sin(fr * (jnp.dot(z, w1_ref[...], precision=hp, preferred_element_type=F32) + b1_ref[...]))
        h = jnp.sin(fr * (jnp.dot(h, w2_ref[...], precision=hp, preferred_element_type=F32) + b2_ref[...]))
        return jnp.sin(fr * (jnp.dot(h, w3_ref[...], precision=hp, preferred_element_type=F32) + b3_ref[...]))

    z = z_ref[...]
    zs = zs_ref[...]
    tl = z.shape[0]
    ff = jnp.dot(mlp(z), w4f_ref[...], precision=hp, preferred_element_type=F32)
    fb = jnp.dot(mlp(zs), w4b_ref[...], precision=hp, preferred_element_type=F32)
    row = pl.program_id(0) * tl + lax.broadcasted_iota(jnp.int32, (tl, 1), 0)
    dec_f = jnp.exp(-z[:, 0:1] * df_ref[...])
    dec_b = jnp.where(row > 0, jnp.exp(-zs[:, 0:1] * db_ref[...]), 0.0)
    c = HY_CH
    o_ref[:, 0 * c:1 * c] = ff[:, :c] * dec_f
    o_ref[:, 1 * c:2 * c] = fb[:, :c] * dec_b
    o_ref[:, 2 * c:3 * c] = ff[:, c:] * dec_f
    o_ref[:, 3 * c:4 * c] = fb[:, c:] * dec_b


def _hyena_filters(z, zs, deltas_f, deltas_b, w1, b1, w2, b2, w3, b3, w4, freq, tl=512):
    seq = z.shape[0]
    k = HF_ORDER
    w1p = jnp.zeros((k, k), F32).at[:HF_EMB].set(w1.astype(F32))
    w4r = w4.astype(F32).reshape(k, 2, 2, HY_CH)
    w4f = w4r[:, :, 0, :].reshape(k, 2 * HY_CH)
    w4b = w4r[:, :, 1, :].reshape(k, 2 * HY_CH)
    row = lambda a: a.reshape(1, -1).astype(F32)
    full = lambda shape: pl.BlockSpec(shape, lambda i: (0, 0))
    return pl.pallas_call(
        _filter_kernel,
        grid=(seq // tl,),
        in_specs=[
            pl.BlockSpec((tl, k), lambda i: (i, 0)), pl.BlockSpec((tl, k), lambda i: (i, 0)),
            full((k, k)), full((1, k)), full((k, k)), full((1, k)), full((k, k)), full((1, k)),
            full((k, 2 * HY_CH)), full((k, 2 * HY_CH)), full((1, k)), full((1, HY_CH)), full((1, HY_CH)),
        ],
        out_specs=pl.BlockSpec((tl, 4 * HY_CH), lambda i: (i, 0)),
        out_shape=jax.ShapeDtypeStruct((seq, 4 * HY_CH), F32),
        compiler_params=_cparams(("parallel",)),
        name="hyena_filters",
    )(z, zs, w1p, row(b1), w2.astype(F32), row(b2), w3.astype(F32), row(b3), w4f, w4b,
      row(freq), deltas_f, deltas_b)


def _dft_tables():
    n2 = np.arange(FFT_R, dtype=np.int64)[:, None, None]
    n1 = np.arange(FFT_HALF, dtype=np.int64)[None, :, None]
    k1 = np.arange(FFT_R, dtype=np.int64)[None, None, :]
    ang = -2.0 * np.pi * ((k1 * (FFT_R * n1 + n2)) % FFT_N) / FFT_N
    gt = np.concatenate([np.cos(ang), np.sin(ang)], axis=-1)
    gtc = np.concatenate([np.cos(ang), -np.sin(ang)], axis=-1)
    k2 = np.arange(FFT_R, dtype=np.int64)[:, None]
    m = np.arange(FFT_R, dtype=np.int64)[None, :]
    a2 = -2.0 * np.pi * ((k2 * m) % FFT_R) / FFT_R
    fr, fi = np.cos(a2), np.sin(a2)
    c2 = np.block([[fr, -fi], [fi, fr]])
    c2c = np.block([[fr, fi], [-fi, fr]])
    return (jnp.asarray(np.stack([gt, gtc]), F32).astype(BF16),
            jnp.asarray(np.stack([c2, c2c]), F32).astype(BF16))


def _dft_stage_a(u_ref, gt_ref, s_ref):
    def body(n2, carry):
        slab = u_ref[pl.ds(n2, FFT_HALF, stride=FFT_R), :].astype(BF16)
        a = lax.dot_general(gt_ref[0, n2], slab, (((0,), (0,)), ((), ())),
                            preferred_element_type=F32)
        s_ref[pl.ds(pl.multiple_of(n2 * 2 * FFT_R, 2 * FFT_R), 2 * FFT_R), :] = a
        return carry
    lax.fori_loop(0, FFT_R, body, 0, unroll=FFT_UNROLL)


def _gather_k1(s_ref, k1):
    re = s_ref[pl.ds(k1, FFT_R, stride=2 * FFT_R), :]
    im = s_ref[pl.ds(k1 + FFT_R, FFT_R, stride=2 * FFT_R), :]
    return jnp.concatenate([re, im], axis=0).astype(BF16)


def _gather_group(s_ref, k0):
    return jnp.concatenate([_gather_k1(s_ref, k0 + t) for t in range(FFT_GROUP)], axis=1)


def _filter_spectrum_kernel(h_ref, gt_ref, c2_ref, o_ref, s_ref, *, kc):
    kstep = pl.program_id(1)

    @pl.when(kstep == 0)
    def _():
        _dft_stage_a(h_ref, gt_ref, s_ref)

    def body(g, carry):
        i0 = g * FFT_GROUP
        a = _gather_group(s_ref, kstep * kc + i0)
        x = jnp.dot(c2_ref[0], a, preferred_element_type=F32)
        for t in range(FFT_GROUP):
            o_ref[0, i0 + t] = x[:, t * LANES:(t + 1) * LANES]
        return carry
    lax.fori_loop(0, kc // FFT_GROUP, body, 0)


def _filter_spectrum(filt, gt_tab, c2_tab, kc=16):
    seq, n_cols = filt.shape
    n_blocks = n_cols // LANES
    per_dir = HY_CH // LANES
    direction = lambda j: (j // per_dir) % 2
    return pl.pallas_call(
        functools.partial(_filter_spectrum_kernel, kc=kc),
        grid=(n_blocks, FFT_R // kc),
        in_specs=[
            pl.BlockSpec((seq, LANES), lambda j, k: (0, j)),
            pl.BlockSpec((1, FFT_R, FFT_HALF, 2 * FFT_R), lambda j, k: (direction(j), 0, 0, 0)),
            pl.BlockSpec((1, 2 * FFT_R, 2 * FFT_R), lambda j, k: (direction(j), 0, 0)),
        ],
        out_specs=pl.BlockSpec((1, kc, 2 * FFT_R, LANES), lambda j, k: (j, k, 0, 0)),
        out_shape=jax.ShapeDtypeStruct((n_blocks, FFT_R, 2 * FFT_R, LANES), F32),
        scratch_shapes=[pltpu.VMEM((FFT_R * 2 * FFT_R, LANES), F32)],
        compiler_params=_cparams(("parallel", "arbitrary")),
        name="filter_spectrum",
    )(filt, gt_tab, c2_tab)


def _hyena_conv_kernel(u_ref, gate_ref, *refs, kc, gated_out):
    if gated_out:
        g2_ref, d_ref, kf_ref, kb_ref, gt_ref, c2_ref, o_ref, s_ref, y_ref = refs
    else:
        d_ref, kf_ref, kb_ref, gt_ref, c2_ref, o_ref, s_ref, y_ref = refs
        g2_ref = None
    kstep = pl.program_id(2)
    seq = u_ref.shape[0]

    @pl.when(kstep == 0)
    def _():
        _dft_stage_a(u_ref, gt_ref, s_ref)

    def body(g, carry):
        i0 = g * FFT_GROUP
        k0 = kstep * kc + i0
        x = jnp.dot(c2_ref[0], _gather_group(s_ref, k0), preferred_element_type=F32)
        ys = []
        for t in range(FFT_GROUP):
            xt = x[:, t * LANES:(t + 1) * LANES]
            kk = kf_ref[0, i0 + t] + kb_ref[0, i0 + t]
            xr, xi = xt[:FFT_R], xt[FFT_R:]
            kr, ki = kk[:FFT_R], kk[FFT_R:]
            ys.append(jnp.concatenate([xr * kr - xi * ki, xr * ki + xi * kr], axis=0).astype(BF16))
        b = jnp.dot(c2_ref[1], jnp.concatenate(ys, axis=1), preferred_element_type=F32)
        for t in range(FFT_GROUP):
            bt = b[:, t * LANES:(t + 1) * LANES]
            s_ref[pl.ds(k0 + t, FFT_R, stride=2 * FFT_R), :] = bt[:FFT_R]
            s_ref[pl.ds(k0 + t + FFT_R, FFT_R, stride=2 * FFT_R), :] = bt[FFT_R:]
        return carry
    lax.fori_loop(0, kc // FFT_GROUP, body, 0)

    @pl.when(kstep == pl.num_programs(2) - 1)
    def _():
        def inv_body(n2, carry):
            b = s_ref[pl.ds(pl.multiple_of(n2 * 2 * FFT_R, 2 * FFT_R), 2 * FFT_R), :].astype(BF16)
            y_ref[pl.ds(n2, FFT_HALF, stride=FFT_R), :] = jnp.dot(gt_ref[0, n2], b,
                                                                  preferred_element_type=F32)
            return carry
        lax.fori_loop(0, FFT_R, inv_body, 0, unroll=FFT_UNROLL)

        rows = 512
        d = d_ref[...]
        for c in range(seq // rows):
            sl = slice(c * rows, (c + 1) * rows)
            r = gate_ref[sl, :] * (y_ref[sl, :] * (1.0 / FFT_N) + u_ref[sl, :] * d)
            if gated_out:
                r = r * _silu(g2_ref[sl, :])
            o_ref[sl, :] = r.astype(o_ref.dtype)


def _hyena_conv(u_arr, u_blk0, gate_arr, gate_blk0, g2, d_skip, spec, order, gt_tab, c2_tab,
                batch, seq, out_dtype, kc=16):
    assert seq == SEQ_LEN
    per_dir = HY_CH // LANES
    once = pl.Buffered(1)
    col = lambda blk0: pl.BlockSpec((seq, LANES), lambda b, c, k: (b, blk0 + c), pipeline_mode=once)
    in_specs = [col(u_blk0), col(gate_blk0)]
    args = [u_arr, gate_arr]
    if g2 is not None:
        in_specs.append(col(g2[1]))
        args.append(g2[0])
    kspec = lambda base: pl.BlockSpec((1, kc, 2 * FFT_R, LANES), lambda b, c, k: (base + c, k, 0, 0))
    in_specs += [
        pl.BlockSpec((1, LANES), lambda b, c, k: (0, c)),
        kspec(order * 2 * per_dir), kspec(order * 2 * per_dir + per_dir),
        pl.BlockSpec((1, FFT_R, FFT_HALF, 2 * FFT_R), lambda b, c, k: (0, 0, 0, 0), pipeline_mode=once),
        pl.BlockSpec((2, 2 * FFT_R, 2 * FFT_R), lambda b, c, k: (0, 0, 0)),
    ]
    args += [d_skip.reshape(1, -1).astype(F32), spec, spec, gt_tab, c2_tab]
    return pl.pallas_call(
        functools.partial(_hyena_conv_kernel, kc=kc, gated_out=g2 is not None),
        grid=(batch, per_dir, FFT_R // kc),
        in_specs=in_specs,
        out_specs=pl.BlockSpec((seq, LANES), lambda b, c, k: (b, c)),
        out_shape=jax.ShapeDtypeStruct((batch * seq, HY_CH), out_dtype),
        scratch_shapes=[pltpu.VMEM((FFT_R * 2 * FFT_R, LANES), F32), pltpu.VMEM((seq, LANES), F32)],
        compiler_params=_cparams(("parallel", "parallel", "arbitrary")),
        name="hyena_conv_gated" if g2 is not None else "hyena_conv",
    )(*args)


def _out_kernel(ya_ref, yb_ref, mq_ref, mg_ref, mk_ref, mv_ref, w_ref, x_ref, gf_ref, o_ref, *, final):
    gw = GROUP_W
    acc = x_ref[...]
    acc += jnp.dot(ya_ref[...], w_ref[0:gw, :], preferred_element_type=F32)
    acc += jnp.dot(yb_ref[...], w_ref[gw:2 * gw, :], preferred_element_type=F32)
    hd = MEM_HEAD_DIM
    for h in range(MEM_HEADS):
        sl = slice(h * hd, (h + 1) * hd)
        s = lax.dot_general(mq_ref[:, sl], mk_ref[:, sl], (((1,), (1,)), ((), ())),
                            preferred_element_type=F32) * (hd ** -0.5)
        e = jnp.exp(s - jnp.max(s, axis=-1, keepdims=True))
        p = e / jnp.sum(e, axis=-1, keepdims=True)
        oh = jnp.dot(p.astype(BF16), mv_ref[:, sl], preferred_element_type=F32)
        yc = (oh * _silu(mg_ref[:, sl])).astype(BF16)
        acc += jnp.dot(yc, w_ref[2 * gw + h * hd:2 * gw + (h + 1) * hd, :], preferred_element_type=F32)
    if final:
        ms = jnp.mean(acc * acc, axis=-1, keepdims=True)
        acc = acc * lax.rsqrt(ms + EPS) * gf_ref[...]
    o_ref[...] = acc


def _out_proj(ya, yb, vm, feats, mg_blk, mkv, w_out, x, g_final, final, seq, mem_len, tm=512):
    m, d = x.shape
    gw = GROUP_W
    per_batch = seq // tm
    return pl.pallas_call(
        functools.partial(_out_kernel, final=final),
        grid=(m // tm,),
        in_specs=[
            pl.BlockSpec((tm, gw), lambda i: (i, 0)),
            pl.BlockSpec((tm, gw), lambda i: (i, 0)),
            pl.BlockSpec((tm, gw), lambda i: (i, 1)),
            pl.BlockSpec((tm, gw), lambda i: (i, mg_blk)),
            pl.BlockSpec((mem_len, gw), lambda i: (i // per_batch, 0)),
            pl.BlockSpec((mem_len, gw), lambda i: (i // per_batch, 1)),
            pl.BlockSpec((3 * gw, d), lambda i: (0, 0)),
            pl.BlockSpec((tm, d), lambda i: (i, 0)),
            pl.BlockSpec((1, d), lambda i: (0, 0)),
        ],
        out_specs=pl.BlockSpec((tm, d), lambda i: (i, 0)),
        out_shape=jax.ShapeDtypeStruct((m, d), F32),
        compiler_params=_cparams(("parallel",)),
        name="out_proj_final" if final else "out_proj",
    )(ya, yb, vm, feats, mkv, mkv, w_out, x, g_final.reshape(1, d).astype(F32))


def _rope_tables(seq):
    half = DA_QK_DIM // 2
    pos = jnp.arange(seq, dtype=F32)
    inv_freq = ROPE_THETA ** (-jnp.arange(0, DA_QK_DIM, 2, dtype=F32) / DA_QK_DIM)
    ang = pos[:, None] * inv_freq[None, :]
    cos, sin = jnp.cos(ang), jnp.sin(ang)
    reps = LANES // DA_QK_DIM
    return (jnp.tile(jnp.concatenate([cos, cos], axis=-1), (1, reps)),
            jnp.tile(jnp.concatenate([-sin, sin], axis=-1), (1, reps)))


def _filter_features(seq):
    t = jnp.linspace(0.0, 1.0, seq, dtype=F32)[:, None]
    w = 2.0 * math.pi * jnp.arange(seq, dtype=F32)[:, None] / seq
    f = jnp.linspace(1e-4, HF_BANDS - 1, HF_BANDS, dtype=F32)[None, :]
    z = jnp.concatenate([t, jnp.cos(f * w), -jnp.sin(f * w)], axis=-1)
    z = jnp.pad(z, ((0, 0), (0, HF_ORDER - HF_EMB)))
    zs = jnp.concatenate([z[:1], z[:-1]], axis=0)
    max_decay = math.log(HF_TARGET) / HF_FAST
    min_decay = math.log(HF_TARGET) / HF_SLOW
    deltas = jnp.abs(jnp.linspace(min_decay, max_decay, HY_CH, dtype=F32))
    return z, zs, deltas[None, :], deltas[::-1][None, :]


def kernel(x, mem, g_norm, w_in, da_lam_q1, da_lam_k1, da_lam_q2, da_lam_k2, da_subln_g, hy_conv_w, hy_conv_b, hf_w1, hf_b1, hf_w2, hf_b2, hf_w3, hf_b3, hf_w4, hf_freq, hy_skip, g_mem, w_mem_kv, w_out, g_final):
    batch, seq, d = x.shape
    mem_len = mem.shape[1]
    depth = w_in.shape[0]
    gw = GROUP_W
    xf = x.reshape(batch * seq, d).astype(F32)
    memf = mem.reshape(batch * mem_len, d).astype(F32)
    rope_tabs = _rope_tables(seq)
    z, zs, deltas_f, deltas_b = _filter_features(seq)
    gt_tab, c2_tab = _dft_tables()

    for l in range(depth):
        w = w_in[l].astype(BF16)
        w_qk = w[:, :2 * gw]
        w_vm = jnp.concatenate([w[:, 2 * gw:3 * gw], w[:, 8 * gw:9 * gw]], axis=1)
        w_ft = jnp.concatenate([w[:, 3 * gw:8 * gw], w[:, 9 * gw:]], axis=1)

        qk = _norm_matmul(xf, g_norm[l], w_qk, BF16, "proj_qk_rope", rope_tabs=rope_tabs,
                          q_scale=DA_QK_DIM ** -0.5 * math.log2(math.e))
        vm = _norm_matmul(xf, g_norm[l], w_vm, BF16, "proj_v_memq")
        feats = _norm_matmul(xf, g_norm[l], w_ft, F32, "proj_features")
        mkv = _norm_matmul(memf, g_mem[l], w_mem_kv[l].astype(BF16), BF16, "proj_mem_kv")

        lam_init = 0.8 - 0.6 * math.exp(-0.3 * l)
        ya = _diff_attention(qk, vm, feats, da_subln_g[l], da_lam_q1[l], da_lam_k1[l],
                             da_lam_q2[l], da_lam_k2[l], lam_init, batch, seq)

        blk = gw // LANES
        u = _short_conv(feats, blk, hy_conv_w[l], hy_conv_b[l], batch, seq)
        filt = _hyena_filters(z, zs, deltas_f, deltas_b, hf_w1[l], hf_b1[l], hf_w2[l], hf_b2[l],
                              hf_w3[l], hf_b3[l], hf_w4[l], hf_freq[l])
        spec = _filter_spectrum(filt, gt_tab, c2_tab)
        z1 = _hyena_conv(u, 0, u, blk, None, hy_skip[l, 0], spec, 0, gt_tab, c2_tab, batch, seq, F32)
        yb = _hyena_conv(z1, 0, u, 2 * blk, (feats, 4 * blk), hy_skip[l, 1], spec, 1, gt_tab, c2_tab,
                         batch, seq, BF16)

        xf = _out_proj(ya, yb, vm, feats, 5, mkv, w_out[l].astype(BF16), xf, g_final,
                       l == depth - 1, seq, mem_len)
    return xf.reshape(batch, seq, d)
```

```python
import functools
import math

import jax
import jax.numpy as jnp
import numpy as np
from jax import lax
from jax.experimental import pallas as pl
from jax.experimental.pallas import tpu as pltpu

F32 = jnp.float32
BF16 = jnp.bfloat16

GROUP_W = 512
DA_HEADS = 4
DA_QK_DIM = 64
DA_V_DIM = 128
HY_CH = 512
HF_EMB = 33
HF_BANDS = 16
HF_ORDER = 64
HF_TARGET = 1e-2
HF_FAST = 0.3
HF_SLOW = 1.5
MEM_HEADS = 4
MEM_HEAD_DIM = 128
ROPE_THETA = 10000.0
EPS = 1e-6

LANES = 128
BF16_SUBLANES = 16
VMEM_LIMIT = 56 * 1024 * 1024

FFT_R = 128
FFT_N = FFT_R * FFT_R
SEQ_LEN = FFT_N // 2
FFT_HALF = FFT_R // 2
HY_CBLK = 8


def _cparams(sem):
    return pltpu.CompilerParams(dimension_semantics=sem, vmem_limit_bytes=VMEM_LIMIT)


def _silu(g):
    return g * (1.0 / (1.0 + jnp.exp(-g)))


def _rms_norm_rows(x, g):
    ms = jnp.mean(x * x, axis=-1, keepdims=True)
    return x * lax.rsqrt(ms + EPS) * g


def _norm_matmul_kernel(x_ref, g_ref, w_ref, *refs, rope, q_scale):
    if rope:
        cos_ref, sin_ref, o_ref, h_ref = refs
    else:
        o_ref, h_ref = refs
    j = pl.program_id(1)

    @pl.when(j == 0)
    def _():
        h_ref[...] = _rms_norm_rows(x_ref[...], g_ref[...]).astype(BF16)

    acc = jnp.dot(h_ref[...], w_ref[...], preferred_element_type=F32)
    if rope:
        tm, tn = acc.shape
        scale = jnp.where(j == 0, q_scale, 1.0).astype(F32)
        cos = cos_ref[...]
        sin = sin_ref[...]
        lane = lax.broadcasted_iota(jnp.int32, (tm, LANES), 1)
        first_half = (lane % DA_QK_DIM) < (DA_QK_DIM // 2)
        for c in range(tn // LANES):
            a = acc[:, c * LANES:(c + 1) * LANES]
            partner = jnp.where(first_half,
                                pltpu.roll(a, LANES - DA_QK_DIM // 2, 1),
                                pltpu.roll(a, DA_QK_DIM // 2, 1))
            o_ref[:, c * LANES:(c + 1) * LANES] = ((a * cos + partner * sin) * scale).astype(o_ref.dtype)
    else:
        o_ref[...] = acc.astype(o_ref.dtype)


def _norm_matmul(x, g, w, out_dtype, name, rope_tabs=None, q_scale=1.0, tm=1024, tn=512):
    m, d = x.shape
    n = w.shape[1]
    tm = min(tm, m)
    assert m % tm == 0 and n % tn == 0
    in_specs = [
        pl.BlockSpec((tm, d), lambda i, j: (i, 0)),
        pl.BlockSpec((1, d), lambda i, j: (0, 0)),
        pl.BlockSpec((d, tn), lambda i, j: (0, j)),
    ]
    args = [x, g.reshape(1, d), w]
    if rope_tabs is not None:
        n_pos = rope_tabs[0].shape[0] // tm
        for t in rope_tabs:
            in_specs.append(pl.BlockSpec((tm, LANES), lambda i, j: (i % n_pos, 0)))
            args.append(t)
    return pl.pallas_call(
        functools.partial(_norm_matmul_kernel, rope=rope_tabs is not None, q_scale=q_scale),
        grid=(m // tm, n // tn),
        in_specs=in_specs,
        out_specs=pl.BlockSpec((tm, tn), lambda i, j: (i, j)),
        out_shape=jax.ShapeDtypeStruct((m, n), out_dtype),
        scratch_shapes=[pltpu.VMEM((tm, d), BF16)],
        compiler_params=_cparams(("parallel", "arbitrary")),
        name=name,
    )(*args)


def _norm_matmul_t_kernel(x_ref, g_ref, wt_ref, o_ref, h_ref):
    @pl.when(pl.program_id(1) == 0)
    def _():
        h_ref[...] = _rms_norm_rows(x_ref[...], g_ref[...]).astype(BF16)

    o_ref[0] = lax.dot_general(wt_ref[...], h_ref[...], (((1,), (1,)), ((), ())),
                               preferred_element_type=F32)


def _norm_matmul_t(x, g, w_t, batch, seq, name, tm=1024, tn=512):
    m, d = x.shape
    n = w_t.shape[0]
    per_batch = seq // tm
    return pl.pallas_call(
        _norm_matmul_t_kernel,
        grid=(m // tm, n // tn),
        in_specs=[
            pl.BlockSpec((tm, d), lambda i, j: (i, 0)),
            pl.BlockSpec((1, d), lambda i, j: (0, 0)),
            pl.BlockSpec((tn, d), lambda i, j: (j, 0)),
        ],
        out_specs=pl.BlockSpec((1, tn, tm), lambda i, j: (i // per_batch, j, i % per_batch)),
        out_shape=jax.ShapeDtypeStruct((batch, n, seq), F32),
        scratch_shapes=[pltpu.VMEM((tm, d), BF16)],
        compiler_params=_cparams(("parallel", "arbitrary")),
        name=name,
    )(x, g.reshape(1, d), w_t)


def _diff_attn_kernel(q_ref, k_ref, v_ref, gate_ref, gsub_ref, lq1_ref, lk1_ref, lq2_ref, lk2_ref,
                      o_ref, vt_ref, m_ref, acc_ref, s_a, s_b, cm_a, cm_b, p_a, p_b,
                      *, lam_init, tq, tk, t_chunk):
    seq = k_ref.shape[0]
    dv = DA_V_DIM
    assert (seq // tk) % 2 == 0 and seq // tk >= 2

    @pl.when(pl.program_id(2) == 0)
    def _():
        row = lax.broadcasted_iota(jnp.int32, (BF16_SUBLANES, t_chunk), 0)
        ones_row = jnp.where(row == 0, 1.0, 0.0).astype(BF16)
        for c in range(seq // t_chunk):
            blk = v_ref[c * t_chunk:(c + 1) * t_chunk, :].astype(F32)
            vt_ref[0:dv, c * t_chunk:(c + 1) * t_chunk] = blk.T.astype(BF16)
            vt_ref[dv:dv + BF16_SUBLANES, c * t_chunk:(c + 1) * t_chunk] = ones_row

    q = q_ref[...]
    lane = lax.broadcasted_iota(jnp.int32, q.shape, 1)
    zero = jnp.zeros_like(q)
    q2 = jnp.concatenate([jnp.where(lane < DA_QK_DIM, q, zero),
                          jnp.where(lane >= DA_QK_DIM, q, zero)], axis=0)

    m_ref[...] = jnp.full(m_ref.shape, -1e30, F32)
    acc_ref[...] = jnp.zeros(acc_ref.shape, F32)

    def scores(c, s_ref, cm_ref):
        kc = k_ref[pl.ds(pl.multiple_of(c * tk, tk), tk), :]
        s = lax.dot_general(kc, q2, (((1,), (1,)), ((), ())), preferred_element_type=F32)
        s_ref[...] = s
        cm_ref[...] = jnp.max(s, axis=0, keepdims=True)

    def weighted_values(c, p_ref):
        vtc = vt_ref[:, pl.ds(pl.multiple_of(c * tk, tk), tk)]
        return jnp.dot(vtc, p_ref[...], preferred_element_type=F32)

    def softmax_chunk(s_ref, cm_ref, p_ref):
        m_prev = m_ref[...]
        m_new = jnp.maximum(m_prev, cm_ref[...])
        p_ref[...] = jnp.exp2(s_ref[...] - m_new).astype(BF16)
        m_ref[...] = m_new
        return jnp.exp2(m_prev - m_new)

    def step(c, s_cur, cm_cur, s_next, cm_next, p_cur, p_prev):
        scores(c + 1, s_next, cm_next)
        pending = weighted_values(c - 1, p_prev)
        alpha = softmax_chunk(s_cur, cm_cur, p_cur)
        acc_ref[...] = (acc_ref[...] + pending) * alpha

    n_chunks = seq // tk
    scores(0, s_a, cm_a)
    scores(1, s_b, cm_b)
    softmax_chunk(s_a, cm_a, p_a)

    def pair(j, carry):
        c = 1 + 2 * j
        step(c, s_b, cm_b, s_a, cm_a, p_b, p_a)
        step(c + 1, s_a, cm_a, s_b, cm_b, p_a, p_b)
        return carry
    lax.fori_loop(0, (n_chunks - 2) // 2, pair, 0)

    last = n_chunks - 1
    pending = weighted_values(last - 1, p_a)
    alpha = softmax_chunk(s_b, cm_b, p_b)
    acc = (acc_ref[...] + pending) * alpha + weighted_values(last, p_b)

    lam = (jnp.exp(jnp.sum(lq1_ref[...] * lk1_ref[...], axis=-1, keepdims=True))
           - jnp.exp(jnp.sum(lq2_ref[...] * lk2_ref[...], axis=-1, keepdims=True)) + lam_init)
    on = acc[0:dv] / acc[dv:dv + 1]
    ot = on[:, :tq] - lam * on[:, tq:]
    o = ot.T
    y = _rms_norm_rows(o, gsub_ref[...]) * (1.0 - lam_init)
    o_ref[...] = (y * _silu(gate_ref[...])).astype(o_ref.dtype)


def _diff_attention(qk, vm, gates, g_sub, lq1, lk1, lq2, lk2, lam_init, batch, seq, tq=512, tk=512):
    nq = seq // tq
    head_blocks = GROUP_W // LANES
    vec = lambda a: a.reshape(1, -1).astype(F32)
    small = lambda n: pl.BlockSpec((1, n), lambda b, h, i: (0, 0))
    return pl.pallas_call(
        functools.partial(_diff_attn_kernel, lam_init=lam_init, tq=tq, tk=tk, t_chunk=512),
        grid=(batch, DA_HEADS, nq),
        in_specs=[
            pl.BlockSpec((tq, LANES), lambda b, h, i: (b * nq + i, h)),
            pl.BlockSpec((seq, LANES), lambda b, h, i: (b, head_blocks + h)),
            pl.BlockSpec((seq, LANES), lambda b, h, i: (b, h)),
            pl.BlockSpec((tq, LANES), lambda b, h, i: (b * nq + i, h)),
            small(DA_V_DIM), small(DA_QK_DIM), small(DA_QK_DIM), small(DA_QK_DIM), small(DA_QK_DIM),
        ],
        out_specs=pl.BlockSpec((tq, LANES), lambda b, h, i: (b * nq + i, h)),
        out_shape=jax.ShapeDtypeStruct((batch * seq, GROUP_W), BF16),
        scratch_shapes=[
            pltpu.VMEM((DA_V_DIM + BF16_SUBLANES, seq), BF16),
            pltpu.VMEM((1, 2 * tq), F32),
            pltpu.VMEM((DA_V_DIM + BF16_SUBLANES, 2 * tq), F32),
            pltpu.VMEM((tk, 2 * tq), F32), pltpu.VMEM((tk, 2 * tq), F32),
            pltpu.VMEM((1, 2 * tq), F32), pltpu.VMEM((1, 2 * tq), F32),
            pltpu.VMEM((tk, 2 * tq), BF16), pltpu.VMEM((tk, 2 * tq), BF16),
        ],
        compiler_params=_cparams(("parallel", "parallel", "arbitrary")),
        name="diff_attention",
    )(qk, qk, vm, gates, vec(g_sub), vec(lq1), vec(lk1), vec(lq2), vec(lk2))


def _filter_kernel(z_ref, zs_ref, w1_ref, b1_ref, w2_ref, b2_ref, w3_ref, b3_ref, w4f_ref, w4b_ref,
                   fr_ref, df_ref, db_ref, o_ref):
    hp = lax.Precision.HIGHEST
    fr = fr_ref[...]

    def mlp(z):
        h = jnp.sin(fr * (jnp.dot(w1_ref[...], z, precision=hp, preferred_element_type=F32) + b1_ref[...]))
        h = jnp.sin(fr * (jnp.dot(w2_ref[...], h, precision=hp, preferred_element_type=F32) + b2_ref[...]))
        return jnp.sin(fr * (jnp.dot(w3_ref[...], h, precision=hp, preferred_element_type=F32) + b3_ref[...]))

    z = z_ref[...]
    zs = zs_ref[...]
    tl = z.shape[1]
    ff = jnp.dot(w4f_ref[...], mlp(z), precision=hp, preferred_element_type=F32)
    fb = jnp.dot(w4b_ref[...], mlp(zs), precision=hp, preferred_element_type=F32)
    col = pl.program_id(0) * tl + lax.broadcasted_iota(jnp.int32, (1, tl), 1)
    dec_f = jnp.exp(-df_ref[...] * z[0:1, :])
    dec_b = jnp.where(col > 0, jnp.exp(-db_ref[...] * zs[0:1, :]), 0.0)
    c = HY_CH
    o_ref[0 * c:1 * c, :] = ff[:c] * dec_f
    o_ref[1 * c:2 * c, :] = fb[:c] * dec_b
    o_ref[2 * c:3 * c, :] = ff[c:] * dec_f
    o_ref[3 * c:4 * c, :] = fb[c:] * dec_b


def _hyena_filters(zt, zst, deltas_f, deltas_b, w1, b1, w2, b2, w3, b3, w4, freq, tl=512):
    seq = zt.shape[1]
    k = HF_ORDER
    w1t = jnp.zeros((k, k), F32).at[:, :HF_EMB].set(w1.astype(F32).T)
    w4r = w4.astype(F32).reshape(k, 2, 2, HY_CH)
    w4ft = w4r[:, :, 0, :].reshape(k, 2 * HY_CH).T
    w4bt = w4r[:, :, 1, :].reshape(k, 2 * HY_CH).T
    colv = lambda a: a.reshape(-1, 1).astype(F32)
    full = lambda shape: pl.BlockSpec(shape, lambda i: (0, 0))
    return pl.pallas_call(
        _filter_kernel,
        grid=(seq // tl,),
        in_specs=[
            pl.BlockSpec((k, tl), lambda i: (0, i)), pl.BlockSpec((k, tl), lambda i: (0, i)),
            full((k, k)), full((k, 1)), full((k, k)), full((k, 1)), full((k, k)), full((k, 1)),
            full((2 * HY_CH, k)), full((2 * HY_CH, k)), full((k, 1)), full((HY_CH, 1)), full((HY_CH, 1)),
        ],
        out_specs=pl.BlockSpec((4 * HY_CH, tl), lambda i: (0, i)),
        out_shape=jax.ShapeDtypeStruct((4 * HY_CH, seq), F32),
        compiler_params=_cparams(("parallel",)),
        name="hyena_filters",
    )(zt, zst, w1t, colv(b1), w2.astype(F32).T, colv(b2), w3.astype(F32).T, colv(b3), w4ft, w4bt,
      colv(freq), deltas_f, deltas_b)


def _dft_tables():
    k1 = np.arange(FFT_R, dtype=np.int64)[:, None]
    n1 = np.arange(FFT_HALF, dtype=np.int64)[None, :]
    a1 = -2.0 * np.pi * ((k1 * n1) % FFT_R) / FFT_R
    f1 = np.concatenate([np.cos(a1), np.sin(a1)], axis=0)
    f1_inv = np.concatenate([np.cos(a1).T, np.sin(a1).T], axis=1)
    n2 = np.arange(FFT_R, dtype=np.int64)[None, :]
    at = -2.0 * np.pi * ((k1 * n2) % FFT_N) / FFT_N
    k2 = np.arange(FFT_R, dtype=np.int64)[None, :]
    a2 = -2.0 * np.pi * ((n2.T * k2) % FFT_R) / FFT_R
    fr, fi = np.cos(a2), np.sin(a2)
    c2 = np.block([[fr, fi], [-fi, fr]])
    c2_inv = np.block([[fr, -fi], [fi, fr]])
    as_bf16 = lambda a: jnp.asarray(a, F32).astype(BF16)
    return dict(f1=as_bf16(f1), f1_inv=as_bf16(f1_inv), c2=as_bf16(c2), c2_inv=as_bf16(c2_inv),
                tr=jnp.asarray(np.cos(at), F32), ti=jnp.asarray(np.sin(at), F32))


def _dft_many(xs, f1_ref, c2_ref, tr, ti):
    r = FFT_R
    xx = jnp.concatenate([x.astype(BF16) for x in xs], axis=1)
    a = jnp.dot(f1_ref[...], xx, preferred_element_type=F32)
    rows = []
    for s in range(len(xs)):
        ar, ai = a[:r, s * r:(s + 1) * r], a[r:, s * r:(s + 1) * r]
        rows.append(jnp.concatenate([ar * tr - ai * ti, ar * ti + ai * tr], axis=1).astype(BF16))
    return jnp.dot(jnp.concatenate(rows, axis=0), c2_ref[...], preferred_element_type=F32)


def _idft_many(ys, f1i_ref, c2i_ref, tr, ti):
    r = FFT_R
    y = jnp.concatenate([v.astype(BF16) for v in ys], axis=0)
    b = jnp.dot(y, c2i_ref[...], preferred_element_type=F32)
    outs_r, outs_i = [], []
    for s in range(len(ys)):
        br, bi = b[s * r:(s + 1) * r, :r], b[s * r:(s + 1) * r, r:]
        outs_r.append((br * tr + bi * ti).astype(BF16))
        outs_i.append((bi * tr - br * ti).astype(BF16))
    rhs = jnp.concatenate([jnp.concatenate(outs_r, axis=1), jnp.concatenate(outs_i, axis=1)], axis=0)
    return jnp.dot(f1i_ref[...], rhs, preferred_element_type=F32)


def _filter_spectrum_kernel(hf_ref, hb_ref, f1_ref, c2_ref, tr_ref, ti_ref, o_ref):
    r = FFT_R
    cb = hf_ref.shape[2]
    xs = [ref[0, 0, ci] for ci in range(cb) for ref in (hf_ref, hb_ref)]
    x = _dft_many(xs, f1_ref, c2_ref, tr_ref[...], ti_ref[...])
    for ci in range(cb):
        xf, xb = x[2 * ci * r:(2 * ci + 1) * r], x[(2 * ci + 1) * r:(2 * ci + 2) * r]
        o_ref[0, ci] = jnp.concatenate([xf[:, :r] + xb[:, :r], xf[:, r:] - xb[:, r:]], axis=1)


def _filter_spectrum(filt5, tabs):
    n_ord, _, n_ch, rows, _ = filt5.shape
    cb = HY_CBLK
    const = lambda shape: pl.BlockSpec(shape, lambda o, c: (0, 0))
    return pl.pallas_call(
        _filter_spectrum_kernel,
        grid=(n_ord, n_ch // cb),
        in_specs=[
            pl.BlockSpec((1, 1, cb, rows, LANES), lambda o, c: (o, 0, c, 0, 0)),
            pl.BlockSpec((1, 1, cb, rows, LANES), lambda o, c: (o, 1, c, 0, 0)),
            const((2 * FFT_R, FFT_HALF)), const((2 * FFT_R, 2 * FFT_R)),
            const((FFT_R, FFT_R)), const((FFT_R, FFT_R)),
        ],
        out_specs=pl.BlockSpec((1, cb, FFT_R, 2 * FFT_R), lambda o, c: (o, c, 0, 0)),
        out_shape=jax.ShapeDtypeStruct((n_ord, n_ch, FFT_R, 2 * FFT_R), F32),
        compiler_params=_cparams(("parallel", "parallel")),
        name="filter_spectrum",
    )(filt5, filt5, tabs["f1"], tabs["c2"], tabs["tr"], tabs["ti"])


def _short_conv_mat(u, w0, w1, w2, bias):
    rows, lanes = u.shape
    lane = lax.broadcasted_iota(jnp.int32, u.shape, 1)
    row = lax.broadcasted_iota(jnp.int32, u.shape, 0)
    prev = pltpu.roll(u, 1, 1)
    prev = jnp.where(lane == 0, jnp.where(row == 0, 0.0, pltpu.roll(prev, 1, 0)), prev)
    nxt = pltpu.roll(u, lanes - 1, 1)
    nxt = jnp.where(lane == lanes - 1, jnp.where(row == rows - 1, 0.0, pltpu.roll(nxt, rows - 1, 0)), nxt)
    return w0 * prev + w1 * u + w2 * nxt + bias


def _hyena_conv_kernel(cw_ref, d_ref, u_ref, gate_ref, *refs, conv_in, gated_out, u_ch0, gate_ch0, order_ch0):
    if gated_out:
        g2_ref, k_ref, f1_ref, f1i_ref, c2_ref, c2i_ref, tr_ref, ti_ref, o_ref = refs
    else:
        k_ref, f1_ref, f1i_ref, c2_ref, c2i_ref, tr_ref, ti_ref, o_ref = refs
        g2_ref = None
    r = FFT_R
    tr, ti = tr_ref[...], ti_ref[...]
    batch, cb = u_ref.shape[0], u_ref.shape[1]
    c0 = pl.program_id(0) * cb
    signals = [(ci, b) for ci in range(cb) for b in range(batch)]

    def conv_taps(ref, b, ci, ch):
        return _short_conv_mat(ref[b, ci], cw_ref[0, ch], cw_ref[1, ch], cw_ref[2, ch], cw_ref[3, ch])

    def long_conv_input(ci, b):
        return conv_taps(u_ref, b, ci, u_ch0 + c0 + ci) if conv_in else u_ref[b, ci]

    x = _dft_many([long_conv_input(ci, b) for ci, b in signals], f1_ref, c2_ref, tr, ti)
    ys = []
    for s, (ci, b) in enumerate(signals):
        kk = k_ref[0, ci]
        kr, ki = kk[:, :r], kk[:, r:]
        xr, xi = x[s * r:(s + 1) * r, :r], x[s * r:(s + 1) * r, r:]
        ys.append(jnp.concatenate([xr * kr - xi * ki, xr * ki + xi * kr], axis=1))
    y = _idft_many(ys, f1i_ref, c2i_ref, tr, ti)
    for s, (ci, b) in enumerate(signals):
        gate = conv_taps(gate_ref, b, ci, gate_ch0 + c0 + ci)
        skip = d_ref[0, order_ch0 + c0 + ci] * long_conv_input(ci, b)
        res = gate * (y[:, s * r:(s + 1) * r] * (1.0 / FFT_N) + skip)
        if gated_out:
            res = res * _silu(g2_ref[b, ci])
        o_ref[b, ci] = res.astype(o_ref.dtype)


def _hyena_conv(conv_wb, d_skip, u_arr, u_ch0, gate_arr, gate_ch0, g2, spec, order, tabs, out_dtype, conv_in):
    batch = u_arr.shape[0]
    assert batch == 2 and u_arr.shape[2] == FFT_HALF
    cb = HY_CBLK
    smem = pl.BlockSpec(memory_space=pltpu.SMEM)
    chan = lambda ch0: pl.BlockSpec((batch, cb, FFT_HALF, LANES), lambda c: (0, ch0 // cb + c, 0, 0))
    const = lambda shape: pl.BlockSpec(shape, lambda c: (0, 0))
    in_specs = [smem, smem, chan(u_ch0), chan(gate_ch0)]
    args = [conv_wb, d_skip, u_arr, gate_arr]
    if g2 is not None:
        in_specs.append(chan(g2[1]))
        args.append(g2[0])
    in_specs += [
        pl.BlockSpec((1, cb, FFT_R, 2 * FFT_R), lambda c: (order, c, 0, 0)),
        const((2 * FFT_R, FFT_HALF)), const((FFT_HALF, 2 * FFT_R)),
        const((2 * FFT_R, 2 * FFT_R)), const((2 * FFT_R, 2 * FFT_R)),
        const((FFT_R, FFT_R)), const((FFT_R, FFT_R)),
    ]
    args += [spec, tabs["f1"], tabs["f1_inv"], tabs["c2"], tabs["c2_inv"], tabs["tr"], tabs["ti"]]
    return pl.pallas_call(
        functools.partial(_hyena_conv_kernel, conv_in=conv_in, gated_out=g2 is not None,
                          u_ch0=u_ch0, gate_ch0=gate_ch0, order_ch0=order * HY_CH),
        grid=(HY_CH // cb,),
        in_specs=in_specs,
        out_specs=pl.BlockSpec((batch, cb, FFT_HALF, LANES), lambda c: (0, c, 0, 0)),
        out_shape=jax.ShapeDtypeStruct((batch, HY_CH, FFT_HALF, LANES), out_dtype),
        compiler_params=_cparams(("parallel",)),
        name="hyena_conv_gated" if g2 is not None else "hyena_conv",
    )(*args)


def _out_kernel(ya_ref, ybt_ref, mq_ref, mg_ref, mk_ref, mv_ref, w_ref, x_ref, gf_ref, o_ref, *, final):
    gw = GROUP_W
    acc = x_ref[...]
    acc += jnp.dot(ya_ref[...], w_ref[0:gw, :], preferred_element_type=F32)
    acc += lax.dot_general(ybt_ref[0], w_ref[gw:2 * gw, :], (((0,), (0,)), ((), ())),
                           preferred_element_type=F32)
    hd = MEM_HEAD_DIM
    for h in range(MEM_HEADS):
        sl = slice(h * hd, (h + 1) * hd)
        s = lax.dot_general(mq_ref[:, sl], mk_ref[:, sl], (((1,), (1,)), ((), ())),
                            preferred_element_type=F32) * (hd ** -0.5)
        e = jnp.exp(s - jnp.max(s, axis=-1, keepdims=True))
        p = e / jnp.sum(e, axis=-1, keepdims=True)
        oh = jnp.dot(p.astype(BF16), mv_ref[:, sl], preferred_element_type=F32)
        yc = (oh * _silu(mg_ref[:, sl])).astype(BF16)
        acc += jnp.dot(yc, w_ref[2 * gw + h * hd:2 * gw + (h + 1) * hd, :], preferred_element_type=F32)
    if final:
        acc = _rms_norm_rows(acc, gf_ref[...])
    o_ref[...] = acc


def _out_proj(ya, ybt, vm, feats, mg_blk, mkv, w_out, x, g_final, final, seq, mem_len, tm=512):
    m, d = x.shape
    gw = GROUP_W
    per_batch = seq // tm
    return pl.pallas_call(
        functools.partial(_out_kernel, final=final),
        grid=(m // tm,),
        in_specs=[
            pl.BlockSpec((tm, gw), lambda i: (i, 0)),
            pl.BlockSpec((1, gw, tm), lambda i: (i // per_batch, 0, i % per_batch)),
            pl.BlockSpec((tm, gw), lambda i: (i, 1)),
            pl.BlockSpec((tm, gw), lambda i: (i, mg_blk)),
            pl.BlockSpec((mem_len, gw), lambda i: (i // per_batch, 0)),
            pl.BlockSpec((mem_len, gw), lambda i: (i // per_batch, 1)),
            pl.BlockSpec((3 * gw, d), lambda i: (0, 0)),
            pl.BlockSpec((tm, d), lambda i: (i, 0)),
            pl.BlockSpec((1, d), lambda i: (0, 0)),
        ],
        out_specs=pl.BlockSpec((tm, d), lambda i: (i, 0)),
        out_shape=jax.ShapeDtypeStruct((m, d), F32),
        compiler_params=_cparams(("parallel",)),
        name="out_proj_final" if final else "out_proj",
    )(ya, ybt, vm, feats, mkv, mkv, w_out, x, g_final.reshape(1, d).astype(F32))


def _rope_tables(seq):
    pos = jnp.arange(seq, dtype=F32)
    inv_freq = ROPE_THETA ** (-jnp.arange(0, DA_QK_DIM, 2, dtype=F32) / DA_QK_DIM)
    ang = pos[:, None] * inv_freq[None, :]
    cos, sin = jnp.cos(ang), jnp.sin(ang)
    reps = LANES // DA_QK_DIM
    return (jnp.tile(jnp.concatenate([cos, cos], axis=-1), (1, reps)),
            jnp.tile(jnp.concatenate([-sin, sin], axis=-1), (1, reps)))


def _filter_features(seq):
    t = jnp.linspace(0.0, 1.0, seq, dtype=F32)[:, None]
    w = 2.0 * math.pi * jnp.arange(seq, dtype=F32)[:, None] / seq
    f = jnp.linspace(1e-4, HF_BANDS - 1, HF_BANDS, dtype=F32)[None, :]
    z = jnp.concatenate([t, jnp.cos(f * w), -jnp.sin(f * w)], axis=-1)
    z = jnp.pad(z, ((0, 0), (0, HF_ORDER - HF_EMB)))
    zs = jnp.concatenate([z[:1], z[:-1]], axis=0)
    max_decay = math.log(HF_TARGET) / HF_FAST
    min_decay = math.log(HF_TARGET) / HF_SLOW
    deltas = jnp.abs(jnp.linspace(min_decay, max_decay, HY_CH, dtype=F32))
    return z.T, zs.T, deltas[:, None], deltas[::-1][:, None]


def kernel(x, mem, g_norm, w_in, da_lam_q1, da_lam_k1, da_lam_q2, da_lam_k2, da_subln_g, hy_conv_w, hy_conv_b, hf_w1, hf_b1, hf_w2, hf_b2, hf_w3, hf_b3, hf_w4, hf_freq, hy_skip, g_mem, w_mem_kv, w_out, g_final):
    batch, seq, d = x.shape
    assert seq == SEQ_LEN
    mem_len = mem.shape[1]
    depth = w_in.shape[0]
    gw = GROUP_W
    xf = x.reshape(batch * seq, d).astype(F32)
    memf = mem.reshape(batch * mem_len, d).astype(F32)
    rope_tabs = _rope_tables(seq)
    zt, zst, deltas_f, deltas_b = _filter_features(seq)
    tabs = _dft_tables()

    for l in range(depth):
        w = w_in[l].astype(BF16)
        w_qk = w[:, :2 * gw]
        w_vm = jnp.concatenate([w[:, 2 * gw:3 * gw], w[:, 8 * gw:9 * gw]], axis=1)
        w_gate = jnp.concatenate([w[:, 3 * gw:4 * gw], w[:, 9 * gw:]], axis=1)
        w_hy_t = w[:, 4 * gw:8 * gw].T

        qk = _norm_matmul(xf, g_norm[l], w_qk, BF16, "proj_qk_rope", rope_tabs=rope_tabs,
                          q_scale=DA_QK_DIM ** -0.5 * math.log2(math.e))
        vm = _norm_matmul(xf, g_norm[l], w_vm, BF16, "proj_v_memq")
        gates = _norm_matmul(xf, g_norm[l], w_gate, F32, "proj_gates")
        hy = _norm_matmul_t(xf, g_norm[l], w_hy_t, batch, seq, "proj_hyena")
        hy = hy.reshape(batch, 4 * HY_CH, FFT_HALF, LANES)
        mkv = _norm_matmul(memf, g_mem[l], w_mem_kv[l].astype(BF16), BF16, "proj_mem_kv")

        lam_init = 0.8 - 0.6 * math.exp(-0.3 * l)
        ya = _diff_attention(qk, vm, gates, da_subln_g[l], da_lam_q1[l], da_lam_k1[l],
                             da_lam_q2[l], da_lam_k2[l], lam_init, batch, seq)

        filt = _hyena_filters(zt, zst, deltas_f, deltas_b, hf_w1[l], hf_b1[l], hf_w2[l], hf_b2[l],
                              hf_w3[l], hf_b3[l], hf_w4[l], hf_freq[l])
        spec = _filter_spectrum(filt.reshape(2, 2, HY_CH, FFT_HALF, LANES), tabs)
        conv_wb = jnp.concatenate([hy_conv_w[l], hy_conv_b[l][None, :]], axis=0).astype(F32)
        d_skip = hy_skip[l].reshape(1, -1).astype(F32)
        z1 = _hyena_conv(conv_wb, d_skip, hy, 0, hy, HY_CH, None, spec, 0, tabs, F32, conv_in=True)
        ybt = _hyena_conv(conv_wb, d_skip, z1, 0, hy, 2 * HY_CH, (hy, 3 * HY_CH), spec, 1, tabs, BF16,
                          conv_in=False)
        ybt = ybt.reshape(batch, HY_CH, seq)

        xf = _out_proj(ya, ybt, vm, gates, 1, mkv, w_out[l].astype(BF16), xf, g_final,
                       l == depth - 1, seq, mem_len)
    return xf.reshape(batch, seq, d)
```

```python
import functools
import math

import jax
import jax.numpy as jnp
import numpy as np
from jax import lax
from jax.experimental import pallas as pl
from jax.experimental.pallas import tpu as pltpu

F32 = jnp.float32
BF16 = jnp.bfloat16

GROUP_W = 512
DA_HEADS = 4
DA_QK_DIM = 64
DA_V_DIM = 128
HY_CH = 512
HF_EMB = 33
HF_BANDS = 16
HF_ORDER = 64
HF_TARGET = 1e-2
HF_FAST = 0.3
HF_SLOW = 1.5
MEM_HEADS = 4
MEM_HEAD_DIM = 128
ROPE_THETA = 10000.0
EPS = 1e-6

LANES = 128
BF16_SUBLANES = 16
VMEM_LIMIT = 56 * 1024 * 1024

FFT_R = 128
FFT_N = FFT_R * FFT_R
SEQ_LEN = FFT_N // 2
FFT_HALF = FFT_R // 2
HY_CBLK = 8


def _cparams(sem):
    return pltpu.CompilerParams(dimension_semantics=sem, vmem_limit_bytes=VMEM_LIMIT)


def _silu(g):
    return g * (1.0 / (1.0 + jnp.exp(-g)))


def _rms_norm_rows(x, g):
    ms = jnp.mean(x * x, axis=-1, keepdims=True)
    return x * lax.rsqrt(ms + EPS) * g


def _norm_matmul_kernel(x_ref, g_ref, w_ref, *refs, rope, q_scale):
    if rope:
        cos_ref, sin_ref, o_ref, h_ref = refs
    else:
        o_ref, h_ref = refs
    j = pl.program_id(1)

    @pl.when(j == 0)
    def _():
        h_ref[...] = _rms_norm_rows(x_ref[...], g_ref[...]).astype(BF16)

    acc = jnp.dot(h_ref[...], w_ref[...], preferred_element_type=F32)
    if rope:
        tm, tn = acc.shape
        scale = jnp.where(j == 0, q_scale, 1.0).astype(F32)
        cos = cos_ref[...]
        sin = sin_ref[...]
        lane = lax.broadcasted_iota(jnp.int32, (tm, LANES), 1)
        first_half = (lane % DA_QK_DIM) < (DA_QK_DIM // 2)
        for c in range(tn // LANES):
            a = acc[:, c * LANES:(c + 1) * LANES]
            partner = jnp.where(first_half,
                                pltpu.roll(a, LANES - DA_QK_DIM // 2, 1),
                                pltpu.roll(a, DA_QK_DIM // 2, 1))
            o_ref[:, c * LANES:(c + 1) * LANES] = ((a * cos + partner * sin) * scale).astype(o_ref.dtype)
    else:
        o_ref[...] = acc.astype(o_ref.dtype)


def _norm_matmul(x, g, w, out_dtype, name, rope_tabs=None, q_scale=1.0, tm=1024, tn=512):
    m, d = x.shape
    n = w.shape[1]
    tm = min(tm, m)
    assert m % tm == 0 and n % tn == 0
    in_specs = [
        pl.BlockSpec((tm, d), lambda i, j: (i, 0)),
        pl.BlockSpec((1, d), lambda i, j: (0, 0)),
        pl.BlockSpec((d, tn), lambda i, j: (0, j)),
    ]
    args = [x, g.reshape(1, d), w]
    if rope_tabs is not None:
        n_pos = rope_tabs[0].shape[0] // tm
        for t in rope_tabs:
            in_specs.append(pl.BlockSpec((tm, LANES), lambda i, j: (i % n_pos, 0)))
            args.append(t)
    return pl.pallas_call(
        functools.partial(_norm_matmul_kernel, rope=rope_tabs is not None, q_scale=q_scale),
        grid=(m // tm, n // tn),
        in_specs=in_specs,
        out_specs=pl.BlockSpec((tm, tn), lambda i, j: (i, j)),
        out_shape=jax.ShapeDtypeStruct((m, n), out_dtype),
        scratch_shapes=[pltpu.VMEM((tm, d), BF16)],
        compiler_params=_cparams(("parallel", "arbitrary")),
        name=name,
    )(*args)


PROJ_TN = 512
PROJ_TOKEN_TILES = 6
PROJ_CHANNEL_TILES = 4


def _input_proj_kernel(x_ref, g_ref, w_ref, wt_ref, cos_ref, sin_ref, qk_ref, vm_ref, gate_ref, hy_ref, h_ref,
                       *, q_scale):
    j = pl.program_id(1)
    tn = PROJ_TN

    @pl.when(j == 0)
    def _():
        h_ref[...] = _rms_norm_rows(x_ref[...], g_ref[...]).astype(BF16)

    def token_tile(t):
        return jnp.dot(h_ref[...], w_ref[:, t * tn:(t + 1) * tn], preferred_element_type=F32)

    for t in range(2):
        @pl.when(j == t)
        def _(t=t):
            acc = token_tile(t)
            tm = acc.shape[0]
            scale = q_scale if t == 0 else 1.0
            cos = cos_ref[...] * scale
            sin = sin_ref[...] * scale
            lane = lax.broadcasted_iota(jnp.int32, (tm, LANES), 1)
            first_half = (lane % DA_QK_DIM) < (DA_QK_DIM // 2)
            for c in range(tn // LANES):
                a = acc[:, c * LANES:(c + 1) * LANES]
                partner = jnp.where(first_half,
                                    pltpu.roll(a, LANES - DA_QK_DIM // 2, 1),
                                    pltpu.roll(a, DA_QK_DIM // 2, 1))
                qk_ref[:, c * LANES:(c + 1) * LANES] = (a * cos + partner * sin).astype(qk_ref.dtype)

    for t in range(2, 4):
        @pl.when(j == t)
        def _(t=t):
            vm_ref[...] = token_tile(t).astype(vm_ref.dtype)

    for t in range(4, 6):
        @pl.when(j == t)
        def _(t=t):
            gate_ref[...] = token_tile(t)

    for t in range(PROJ_CHANNEL_TILES):
        @pl.when(j == PROJ_TOKEN_TILES + t)
        def _(t=t):
            hy_ref[0] = lax.dot_general(wt_ref[t * tn:(t + 1) * tn, :], h_ref[...], (((1,), (1,)), ((), ())),
                                        preferred_element_type=F32)


def _input_proj(x, g, w_tok, w_ch_t, rope_tabs, q_scale, batch, seq, tm=1024):
    m, d = x.shape
    tn = PROJ_TN
    assert w_tok.shape == (d, PROJ_TOKEN_TILES * tn) and w_ch_t.shape == (PROJ_CHANNEL_TILES * tn, d)
    per_batch = seq // tm
    once = pl.Buffered(1)
    clip = lambda j, lo, n: jnp.clip(j - lo, 0, n - 1)
    return pl.pallas_call(
        functools.partial(_input_proj_kernel, q_scale=q_scale),
        grid=(m // tm, PROJ_TOKEN_TILES + PROJ_CHANNEL_TILES),
        in_specs=[
            pl.BlockSpec((tm, d), lambda i, j: (i, 0)),
            pl.BlockSpec((1, d), lambda i, j: (0, 0)),
            pl.BlockSpec(w_tok.shape, lambda i, j: (0, 0), pipeline_mode=once),
            pl.BlockSpec(w_ch_t.shape, lambda i, j: (0, 0), pipeline_mode=once),
            pl.BlockSpec((tm, LANES), lambda i, j: (i % per_batch, 0)),
            pl.BlockSpec((tm, LANES), lambda i, j: (i % per_batch, 0)),
        ],
        out_specs=[
            pl.BlockSpec((tm, tn), lambda i, j: (i, clip(j, 0, 2))),
            pl.BlockSpec((tm, tn), lambda i, j: (i, clip(j, 2, 2))),
            pl.BlockSpec((tm, tn), lambda i, j: (i, clip(j, 4, 2))),
            pl.BlockSpec((1, tn, tm), lambda i, j: (i // per_batch, clip(j, PROJ_TOKEN_TILES, PROJ_CHANNEL_TILES),
                                                    i % per_batch)),
        ],
        out_shape=[
            jax.ShapeDtypeStruct((m, 2 * tn), BF16),
            jax.ShapeDtypeStruct((m, 2 * tn), BF16),
            jax.ShapeDtypeStruct((m, 2 * tn), F32),
            jax.ShapeDtypeStruct((batch, PROJ_CHANNEL_TILES * tn, seq), F32),
        ],
        scratch_shapes=[pltpu.VMEM((tm, d), BF16)],
        compiler_params=_cparams(("parallel", "arbitrary")),
        name="input_proj",
    )(x, g.reshape(1, d), w_tok, w_ch_t, rope_tabs[0], rope_tabs[1])


def _diff_attn_kernel(q_ref, k_ref, v_ref, gate_ref, gsub_ref, lq1_ref, lk1_ref, lq2_ref, lk2_ref,
                      o_ref, vt_ref, m_ref, acc_ref, s_a, s_b, cm_a, cm_b, p_a, p_b,
                      *, lam_init, tq, tk, t_chunk):
    seq = k_ref.shape[0]
    dv = DA_V_DIM
    assert (seq // tk) % 2 == 0 and seq // tk >= 2

    @pl.when(pl.program_id(2) == 0)
    def _():
        row = lax.broadcasted_iota(jnp.int32, (BF16_SUBLANES, t_chunk), 0)
        ones_row = jnp.where(row == 0, 1.0, 0.0).astype(BF16)
        for c in range(seq // t_chunk):
            blk = v_ref[c * t_chunk:(c + 1) * t_chunk, :].astype(F32)
            vt_ref[0:dv, c * t_chunk:(c + 1) * t_chunk] = blk.T.astype(BF16)
            vt_ref[dv:dv + BF16_SUBLANES, c * t_chunk:(c + 1) * t_chunk] = ones_row

    q = q_ref[...]
    lane = lax.broadcasted_iota(jnp.int32, q.shape, 1)
    zero = jnp.zeros_like(q)
    q2 = jnp.concatenate([jnp.where(lane < DA_QK_DIM, q, zero),
                          jnp.where(lane >= DA_QK_DIM, q, zero)], axis=0)

    m_ref[...] = jnp.full(m_ref.shape, -1e30, F32)
    acc_ref[...] = jnp.zeros(acc_ref.shape, F32)

    def scores(c, s_ref, cm_ref):
        kc = k_ref[pl.ds(pl.multiple_of(c * tk, tk), tk), :]
        s = lax.dot_general(kc, q2, (((1,), (1,)), ((), ())), preferred_element_type=F32)
        s_ref[...] = s
        cm_ref[...] = jnp.max(s, axis=0, keepdims=True)

    def weighted_values(c, p_ref):
        vtc = vt_ref[:, pl.ds(pl.multiple_of(c * tk, tk), tk)]
        return jnp.dot(vtc, p_ref[...], preferred_element_type=F32)

    def softmax_chunk(s_ref, cm_ref, p_ref):
        m_prev = m_ref[...]
        m_new = jnp.maximum(m_prev, cm_ref[...])
        p_ref[...] = jnp.exp2(s_ref[...] - m_new).astype(BF16)
        m_ref[...] = m_new
        return jnp.exp2(m_prev - m_new)

    def step(c, s_cur, cm_cur, s_next, cm_next, p_cur, p_prev):
        scores(c + 1, s_next, cm_next)
        pending = weighted_values(c - 1, p_prev)
        alpha = softmax_chunk(s_cur, cm_cur, p_cur)
        acc_ref[...] = (acc_ref[...] + pending) * alpha

    n_chunks = seq // tk
    scores(0, s_a, cm_a)
    scores(1, s_b, cm_b)
    softmax_chunk(s_a, cm_a, p_a)

    def pair(j, carry):
        c = 1 + 2 * j
        step(c, s_b, cm_b, s_a, cm_a, p_b, p_a)
        step(c + 1, s_a, cm_a, s_b, cm_b, p_a, p_b)
        return carry
    lax.fori_loop(0, (n_chunks - 2) // 2, pair, 0)

    last = n_chunks - 1
    pending = weighted_values(last - 1, p_a)
    alpha = softmax_chunk(s_b, cm_b, p_b)
    acc = (acc_ref[...] + pending) * alpha + weighted_values(last, p_b)

    lam = (jnp.exp(jnp.sum(lq1_ref[...] * lk1_ref[...], axis=-1, keepdims=True))
           - jnp.exp(jnp.sum(lq2_ref[...] * lk2_ref[...], axis=-1, keepdims=True)) + lam_init)
    on = acc[0:dv] / acc[dv:dv + 1]
    ot = on[:, :tq] - lam * on[:, tq:]
    o = ot.T
    y = _rms_norm_rows(o, gsub_ref[...]) * (1.0 - lam_init)
    o_ref[...] = (y * _silu(gate_ref[...])).astype(o_ref.dtype)


def _diff_attention(qk, vm, gates, g_sub, lq1, lk1, lq2, lk2, lam_init, batch, seq, tq=2048, tk=512):
    nq = seq // tq
    head_blocks = GROUP_W // LANES
    vec = lambda a: a.reshape(1, -1).astype(F32)
    small = lambda n: pl.BlockSpec((1, n), lambda b, h, i: (0, 0))
    return pl.pallas_call(
        functools.partial(_diff_attn_kernel, lam_init=lam_init, tq=tq, tk=tk, t_chunk=512),
        grid=(batch, DA_HEADS, nq),
        in_specs=[
            pl.BlockSpec((tq, LANES), lambda b, h, i: (b * nq + i, h)),
            pl.BlockSpec((seq, LANES), lambda b, h, i: (b, head_blocks + h)),
            pl.BlockSpec((seq, LANES), lambda b, h, i: (b, h)),
            pl.BlockSpec((tq, LANES), lambda b, h, i: (b * nq + i, h)),
            small(DA_V_DIM), small(DA_QK_DIM), small(DA_QK_DIM), small(DA_QK_DIM), small(DA_QK_DIM),
        ],
        out_specs=pl.BlockSpec((tq, LANES), lambda b, h, i: (b * nq + i, h)),
        out_shape=jax.ShapeDtypeStruct((batch * seq, GROUP_W), BF16),
        scratch_shapes=[
            pltpu.VMEM((DA_V_DIM + BF16_SUBLANES, seq), BF16),
            pltpu.VMEM((1, 2 * tq), F32),
            pltpu.VMEM((DA_V_DIM + BF16_SUBLANES, 2 * tq), F32),
            pltpu.VMEM((tk, 2 * tq), F32), pltpu.VMEM((tk, 2 * tq), F32),
            pltpu.VMEM((1, 2 * tq), F32), pltpu.VMEM((1, 2 * tq), F32),
            pltpu.VMEM((tk, 2 * tq), BF16), pltpu.VMEM((tk, 2 * tq), BF16),
        ],
        compiler_params=_cparams(("parallel", "parallel", "arbitrary")),
        name="diff_attention",
    )(qk, qk, vm, gates, vec(g_sub), vec(lq1), vec(lk1), vec(lq2), vec(lk2))


def _filter_kernel(z_ref, zs_ref, w1_ref, b1_ref, w2_ref, b2_ref, w3_ref, b3_ref, w4f_ref, w4b_ref,
                   fr_ref, df_ref, db_ref, o_ref):
    hp = lax.Precision.HIGHEST
    fr = fr_ref[...]

    def mlp(z):
        h = jnp.sin(fr * (jnp.dot(w1_ref[...], z, precision=hp, preferred_element_type=F32) + b1_ref[...]))
        h = jnp.sin(fr * (jnp.dot(w2_ref[...], h, precision=hp, preferred_element_type=F32) + b2_ref[...]))
        return jnp.sin(fr * (jnp.dot(w3_ref[...], h, precision=hp, preferred_element_type=F32) + b3_ref[...]))

    z = z_ref[...]
    zs = zs_ref[...]
    tl = z.shape[1]
    ff = jnp.dot(w4f_ref[...], mlp(z), precision=hp, preferred_element_type=F32)
    fb = jnp.dot(w4b_ref[...], mlp(zs), precision=hp, preferred_element_type=F32)
    col = pl.program_id(0) * tl + lax.broadcasted_iota(jnp.int32, (1, tl), 1)
    dec_f = jnp.exp(-df_ref[...] * z[0:1, :])
    dec_b = jnp.where(col > 0, jnp.exp(-db_ref[...] * zs[0:1, :]), 0.0)
    c = HY_CH
    o_ref[0 * c:1 * c, :] = ff[:c] * dec_f
    o_ref[1 * c:2 * c, :] = fb[:c] * dec_b
    o_ref[2 * c:3 * c, :] = ff[c:] * dec_f
    o_ref[3 * c:4 * c, :] = fb[c:] * dec_b


def _hyena_filters(zt, zst, deltas_f, deltas_b, w1, b1, w2, b2, w3, b3, w4, freq, tl=512):
    seq = zt.shape[1]
    k = HF_ORDER
    w1t = jnp.zeros((k, k), F32).at[:, :HF_EMB].set(w1.astype(F32).T)
    w4r = w4.astype(F32).reshape(k, 2, 2, HY_CH)
    w4ft = w4r[:, :, 0, :].reshape(k, 2 * HY_CH).T
    w4bt = w4r[:, :, 1, :].reshape(k, 2 * HY_CH).T
    colv = lambda a: a.reshape(-1, 1).astype(F32)
    full = lambda shape: pl.BlockSpec(shape, lambda i: (0, 0))
    return pl.pallas_call(
        _filter_kernel,
        grid=(seq // tl,),
        in_specs=[
            pl.BlockSpec((k, tl), lambda i: (0, i)), pl.BlockSpec((k, tl), lambda i: (0, i)),
            full((k, k)), full((k, 1)), full((k, k)), full((k, 1)), full((k, k)), full((k, 1)),
            full((2 * HY_CH, k)), full((2 * HY_CH, k)), full((k, 1)), full((HY_CH, 1)), full((HY_CH, 1)),
        ],
        out_specs=pl.BlockSpec((4 * HY_CH, tl), lambda i: (0, i)),
        out_shape=jax.ShapeDtypeStruct((4 * HY_CH, seq), F32),
        compiler_params=_cparams(("parallel",)),
        name="hyena_filters",
    )(zt, zst, w1t, colv(b1), w2.astype(F32).T, colv(b2), w3.astype(F32).T, colv(b3), w4ft, w4bt,
      colv(freq), deltas_f, deltas_b)


def _dft_tables():
    k1 = np.arange(FFT_R, dtype=np.int64)[:, None]
    n1 = np.arange(FFT_HALF, dtype=np.int64)[None, :]
    a1 = -2.0 * np.pi * ((k1 * n1) % FFT_R) / FFT_R
    f1 = np.concatenate([np.cos(a1), np.sin(a1)], axis=0)
    f1_inv = np.concatenate([np.cos(a1).T, np.sin(a1).T], axis=1)
    n2 = np.arange(FFT_R, dtype=np.int64)[None, :]
    at = -2.0 * np.pi * ((k1 * n2) % FFT_N) / FFT_N
    k2 = np.arange(FFT_R, dtype=np.int64)[None, :]
    a2 = -2.0 * np.pi * ((n2.T * k2) % FFT_R) / FFT_R
    fr, fi = np.cos(a2), np.sin(a2)
    c2 = np.block([[fr, fi], [-fi, fr]])
    c2_inv = np.block([[fr, -fi], [fi, fr]])
    as_bf16 = lambda a: jnp.asarray(a, F32).astype(BF16)
    return dict(f1=as_bf16(f1), f1_inv=as_bf16(f1_inv), c2=as_bf16(c2), c2_inv=as_bf16(c2_inv),
                tr=jnp.asarray(np.cos(at), F32), ti=jnp.asarray(np.sin(at), F32))


def _dft_many(xs, f1_ref, c2_ref, tr, ti):
    r = FFT_R
    xx = jnp.concatenate([x.astype(BF16) for x in xs], axis=1)
    a = jnp.dot(f1_ref[...], xx, preferred_element_type=F32)
    rows = []
    for s in range(len(xs)):
        ar, ai = a[:r, s * r:(s + 1) * r], a[r:, s * r:(s + 1) * r]
        rows.append(jnp.concatenate([ar * tr - ai * ti, ar * ti + ai * tr], axis=1).astype(BF16))
    return jnp.dot(jnp.concatenate(rows, axis=0), c2_ref[...], preferred_element_type=F32)


def _idft_many(ys, f1i_ref, c2i_ref, tr, ti):
    r = FFT_R
    y = jnp.concatenate([v.astype(BF16) for v in ys], axis=0)
    b = jnp.dot(y, c2i_ref[...], preferred_element_type=F32)
    outs_r, outs_i = [], []
    for s in range(len(ys)):
        br, bi = b[s * r:(s + 1) * r, :r], b[s * r:(s + 1) * r, r:]
        outs_r.append((br * tr + bi * ti).astype(BF16))
        outs_i.append((bi * tr - br * ti).astype(BF16))
    rhs = jnp.concatenate([jnp.concatenate(outs_r, axis=1), jnp.concatenate(outs_i, axis=1)], axis=0)
    return jnp.dot(f1i_ref[...], rhs, preferred_element_type=F32)


def _filter_spectrum_kernel(hf_ref, hb_ref, f1_ref, c2_ref, tr_ref, ti_ref, o_ref):
    r = FFT_R
    cb = hf_ref.shape[2]
    xs = [ref[0, 0, ci] for ci in range(cb) for ref in (hf_ref, hb_ref)]
    x = _dft_many(xs, f1_ref, c2_ref, tr_ref[...], ti_ref[...])
    for ci in range(cb):
        xf, xb = x[2 * ci * r:(2 * ci + 1) * r], x[(2 * ci + 1) * r:(2 * ci + 2) * r]
        o_ref[0, ci] = jnp.concatenate([xf[:, :r] + xb[:, :r], xf[:, r:] - xb[:, r:]], axis=1)


def _filter_spectrum(filt5, tabs):
    n_ord, _, n_ch, rows, _ = filt5.shape
    cb = HY_CBLK
    const = lambda shape: pl.BlockSpec(shape, lambda o, c: (0, 0))
    return pl.pallas_call(
        _filter_spectrum_kernel,
        grid=(n_ord, n_ch // cb),
        in_specs=[
            pl.BlockSpec((1, 1, cb, rows, LANES), lambda o, c: (o, 0, c, 0, 0)),
            pl.BlockSpec((1, 1, cb, rows, LANES), lambda o, c: (o, 1, c, 0, 0)),
            const((2 * FFT_R, FFT_HALF)), const((2 * FFT_R, 2 * FFT_R)),
            const((FFT_R, FFT_R)), const((FFT_R, FFT_R)),
        ],
        out_specs=pl.BlockSpec((1, cb, FFT_R, 2 * FFT_R), lambda o, c: (o, c, 0, 0)),
        out_shape=jax.ShapeDtypeStruct((n_ord, n_ch, FFT_R, 2 * FFT_R), F32),
        compiler_params=_cparams(("parallel", "parallel")),
        name="filter_spectrum",
    )(filt5, filt5, tabs["f1"], tabs["c2"], tabs["tr"], tabs["ti"])


def _short_conv_mat(u, w0, w1, w2, bias):
    rows, lanes = u.shape
    lane = lax.broadcasted_iota(jnp.int32, u.shape, 1)
    row = lax.broadcasted_iota(jnp.int32, u.shape, 0)
    prev = pltpu.roll(u, 1, 1)
    prev = jnp.where(lane == 0, jnp.where(row == 0, 0.0, pltpu.roll(prev, 1, 0)), prev)
    nxt = pltpu.roll(u, lanes - 1, 1)
    nxt = jnp.where(lane == lanes - 1, jnp.where(row == rows - 1, 0.0, pltpu.roll(nxt, rows - 1, 0)), nxt)
    return w0 * prev + w1 * u + w2 * nxt + bias


def _hyena_conv_kernel(cw_ref, d_ref, u_ref, gate_ref, *refs, conv_in, gated_out, u_ch0, gate_ch0, order_ch0):
    if gated_out:
        g2_ref, k_ref, f1_ref, f1i_ref, c2_ref, c2i_ref, tr_ref, ti_ref, o_ref = refs
    else:
        k_ref, f1_ref, f1i_ref, c2_ref, c2i_ref, tr_ref, ti_ref, o_ref = refs
        g2_ref = None
    r = FFT_R
    tr, ti = tr_ref[...], ti_ref[...]
    batch, cb = u_ref.shape[0], u_ref.shape[1]
    c0 = pl.program_id(0) * cb
    signals = [(ci, b) for ci in range(cb) for b in range(batch)]

    def conv_taps(ref, b, ci, ch):
        return _short_conv_mat(ref[b, ci], cw_ref[0, ch], cw_ref[1, ch], cw_ref[2, ch], cw_ref[3, ch])

    def long_conv_input(ci, b):
        return conv_taps(u_ref, b, ci, u_ch0 + c0 + ci) if conv_in else u_ref[b, ci]

    x = _dft_many([long_conv_input(ci, b) for ci, b in signals], f1_ref, c2_ref, tr, ti)
    ys = []
    for s, (ci, b) in enumerate(signals):
        kk = k_ref[0, ci]
        kr, ki = kk[:, :r], kk[:, r:]
        xr, xi = x[s * r:(s + 1) * r, :r], x[s * r:(s + 1) * r, r:]
        ys.append(jnp.concatenate([xr * kr - xi * ki, xr * ki + xi * kr], axis=1))
    y = _idft_many(ys, f1i_ref, c2i_ref, tr, ti)
    for s, (ci, b) in enumerate(signals):
        gate = conv_taps(gate_ref, b, ci, gate_ch0 + c0 + ci)
        skip = d_ref[0, order_ch0 + c0 + ci] * long_conv_input(ci, b)
        res = gate * (y[:, s * r:(s + 1) * r] * (1.0 / FFT_N) + skip)
        if gated_out:
            res = res * _silu(g2_ref[b, ci])
        o_ref[b, ci] = res.astype(o_ref.dtype)


def _hyena_conv(conv_wb, d_skip, u_arr, u_ch0, gate_arr, gate_ch0, g2, spec, order, tabs, out_dtype, conv_in):
    batch = u_arr.shape[0]
    assert batch == 2 and u_arr.shape[2] == FFT_HALF
    cb = HY_CBLK
    smem = pl.BlockSpec(memory_space=pltpu.SMEM)
    chan = lambda ch0: pl.BlockSpec((batch, cb, FFT_HALF, LANES), lambda c: (0, ch0 // cb + c, 0, 0))
    const = lambda shape: pl.BlockSpec(shape, lambda c: (0, 0))
    in_specs = [smem, smem, chan(u_ch0), chan(gate_ch0)]
    args = [conv_wb, d_skip, u_arr, gate_arr]
    if g2 is not None:
        in_specs.append(chan(g2[1]))
        args.append(g2[0])
    in_specs += [
        pl.BlockSpec((1, cb, FFT_R, 2 * FFT_R), lambda c: (order, c, 0, 0)),
        const((2 * FFT_R, FFT_HALF)), const((FFT_HALF, 2 * FFT_R)),
        const((2 * FFT_R, 2 * FFT_R)), const((2 * FFT_R, 2 * FFT_R)),
        const((FFT_R, FFT_R)), const((FFT_R, FFT_R)),
    ]
    args += [spec, tabs["f1"], tabs["f1_inv"], tabs["c2"], tabs["c2_inv"], tabs["tr"], tabs["ti"]]
    return pl.pallas_call(
        functools.partial(_hyena_conv_kernel, conv_in=conv_in, gated_out=g2 is not None,
                          u_ch0=u_ch0, gate_ch0=gate_ch0, order_ch0=order * HY_CH),
        grid=(HY_CH // cb,),
        in_specs=in_specs,
        out_specs=pl.BlockSpec((batch, cb, FFT_HALF, LANES), lambda c: (0, c, 0, 0)),
        out_shape=jax.ShapeDtypeStruct((batch, HY_CH, FFT_HALF, LANES), out_dtype),
        compiler_params=_cparams(("parallel",)),
        name="hyena_conv_gated" if g2 is not None else "hyena_conv",
    )(*args)


def _out_kernel(ya_ref, ybt_ref, mq_ref, mg_ref, mk_ref, mv_ref, w_ref, x_ref, gf_ref, o_ref, *, final):
    gw = GROUP_W
    acc = x_ref[...]
    acc += jnp.dot(ya_ref[...], w_ref[0:gw, :], preferred_element_type=F32)
    acc += lax.dot_general(ybt_ref[0], w_ref[gw:2 * gw, :], (((0,), (0,)), ((), ())),
                           preferred_element_type=F32)
    hd = MEM_HEAD_DIM
    heads = []
    for h in range(MEM_HEADS):
        sl = slice(h * hd, (h + 1) * hd)
        s = lax.dot_general(mq_ref[:, sl], mk_ref[:, sl], (((1,), (1,)), ((), ())),
                            preferred_element_type=F32) * (hd ** -0.5)
        e = jnp.exp(s - jnp.max(s, axis=-1, keepdims=True))
        p = e / jnp.sum(e, axis=-1, keepdims=True)
        oh = jnp.dot(p.astype(BF16), mv_ref[:, sl], preferred_element_type=F32)
        heads.append((oh * _silu(mg_ref[:, sl])).astype(BF16))
    acc += jnp.dot(jnp.concatenate(heads, axis=1), w_ref[2 * gw:3 * gw, :], preferred_element_type=F32)
    if final:
        acc = _rms_norm_rows(acc, gf_ref[...])
    o_ref[...] = acc


def _out_proj(ya, ybt, vm, feats, mg_blk, mkv, w_out, x, g_final, final, seq, mem_len, tm=512):
    m, d = x.shape
    gw = GROUP_W
    per_batch = seq // tm
    return pl.pallas_call(
        functools.partial(_out_kernel, final=final),
        grid=(m // tm,),
        in_specs=[
            pl.BlockSpec((tm, gw), lambda i: (i, 0)),
            pl.BlockSpec((1, gw, tm), lambda i: (i // per_batch, 0, i % per_batch)),
            pl.BlockSpec((tm, gw), lambda i: (i, 1)),
            pl.BlockSpec((tm, gw), lambda i: (i, mg_blk)),
            pl.BlockSpec((mem_len, gw), lambda i: (i // per_batch, 0)),
            pl.BlockSpec((mem_len, gw), lambda i: (i // per_batch, 1)),
            pl.BlockSpec((3 * gw, d), lambda i: (0, 0)),
            pl.BlockSpec((tm, d), lambda i: (i, 0)),
            pl.BlockSpec((1, d), lambda i: (0, 0)),
        ],
        out_specs=pl.BlockSpec((tm, d), lambda i: (i, 0)),
        out_shape=jax.ShapeDtypeStruct((m, d), F32),
        compiler_params=_cparams(("parallel",)),
        name="out_proj_final" if final else "out_proj",
    )(ya, ybt, vm, feats, mkv, mkv, w_out, x, g_final.reshape(1, d).astype(F32))


def _rope_tables(seq):
    pos = jnp.arange(seq, dtype=F32)
    inv_freq = ROPE_THETA ** (-jnp.arange(0, DA_QK_DIM, 2, dtype=F32) / DA_QK_DIM)
    ang = pos[:, None] * inv_freq[None, :]
    cos, sin = jnp.cos(ang), jnp.sin(ang)
    reps = LANES // DA_QK_DIM
    return (jnp.tile(jnp.concatenate([cos, cos], axis=-1), (1, reps)),
            jnp.tile(jnp.concatenate([-sin, sin], axis=-1), (1, reps)))


def _filter_features(seq):
    t = jnp.linspace(0.0, 1.0, seq, dtype=F32)[:, None]
    w = 2.0 * math.pi * jnp.arange(seq, dtype=F32)[:, None] / seq
    f = jnp.linspace(1e-4, HF_BANDS - 1, HF_BANDS, dtype=F32)[None, :]
    z = jnp.concatenate([t, jnp.cos(f * w), -jnp.sin(f * w)], axis=-1)
    z = jnp.pad(z, ((0, 0), (0, HF_ORDER - HF_EMB)))
    zs = jnp.concatenate([z[:1], z[:-1]], axis=0)
    max_decay = math.log(HF_TARGET) / HF_FAST
    min_decay = math.log(HF_TARGET) / HF_SLOW
    deltas = jnp.abs(jnp.linspace(min_decay, max_decay, HY_CH, dtype=F32))
    return z.T, zs.T, deltas[:, None], deltas[::-1][:, None]


def kernel(x, mem, g_norm, w_in, da_lam_q1, da_lam_k1, da_lam_q2, da_lam_k2, da_subln_g, hy_conv_w, hy_conv_b, hf_w1, hf_b1, hf_w2, hf_b2, hf_w3, hf_b3, hf_w4, hf_freq, hy_skip, g_mem, w_mem_kv, w_out, g_final):
    batch, seq, d = x.shape
    assert seq == SEQ_LEN
    mem_len = mem.shape[1]
    depth = w_in.shape[0]
    gw = GROUP_W
    xf = x.reshape(batch * seq, d).astype(F32)
    memf = mem.reshape(batch * mem_len, d).astype(F32)
    rope_tabs = _rope_tables(seq)
    zt, zst, deltas_f, deltas_b = _filter_features(seq)
    tabs = _dft_tables()

    for l in range(depth):
        w = w_in[l].astype(BF16)
        w_tok = jnp.concatenate([w[:, :3 * gw], w[:, 8 * gw:9 * gw], w[:, 3 * gw:4 * gw], w[:, 9 * gw:]], axis=1)
        w_hy_t = w[:, 4 * gw:8 * gw].T
        qk, vm, gates, hy = _input_proj(xf, g_norm[l], w_tok, w_hy_t, rope_tabs,
                                        DA_QK_DIM ** -0.5 * math.log2(math.e),
                                        batch, seq)
        hy = hy.reshape(batch, 4 * HY_CH, FFT_HALF, LANES)
        mkv = _norm_matmul(memf, g_mem[l], w_mem_kv[l].astype(BF16), BF16, "proj_mem_kv")

        lam_init = 0.8 - 0.6 * math.exp(-0.3 * l)
        ya = _diff_attention(qk, vm, gates, da_subln_g[l], da_lam_q1[l], da_lam_k1[l],
                             da_lam_q2[l], da_lam_k2[l], lam_init, batch, seq)

        filt = _hyena_filters(zt, zst, deltas_f, deltas_b, hf_w1[l], hf_b1[l], hf_w2[l], hf_b2[l],
                              hf_w3[l], hf_b3[l], hf_w4[l], hf_freq[l])
        spec = _filter_spectrum(filt.reshape(2, 2, HY_CH, FFT_HALF, LANES), tabs)
        conv_wb = jnp.concatenate([hy_conv_w[l], hy_conv_b[l][None, :]], axis=0).astype(F32)
        d_skip = hy_skip[l].reshape(1, -1).astype(F32)
        z1 = _hyena_conv(conv_wb, d_skip, hy, 0, hy, HY_CH, None, spec, 0, tabs, F32, conv_in=True)
        ybt = _hyena_conv(conv_wb, d_skip, z1, 0, hy, 2 * HY_CH, (hy, 3 * HY_CH), spec, 1, tabs, BF16,
                          conv_in=False)
        ybt = ybt.reshape(batch, HY_CH, seq)

        xf = _out_proj(ya, ybt, vm, gates, 1, mkv, w_out[l].astype(BF16), xf, g_final,
                       l == depth - 1, seq, mem_len)
    return xf.reshape(batch, seq, d)
```

```python
import functools
import math

import jax
import jax.numpy as jnp
import numpy as np
from jax import lax
from jax.experimental import pallas as pl
from jax.experimental.pallas import tpu as pltpu

F32 = jnp.float32
BF16 = jnp.bfloat16

GROUP_W = 512
DA_HEADS = 4
DA_QK_DIM = 64
DA_V_DIM = 128
HY_CH = 512
HF_EMB = 33
HF_BANDS = 16
HF_ORDER = 64
HF_TARGET = 1e-2
HF_FAST = 0.3
HF_SLOW = 1.5
MEM_HEADS = 4
MEM_HEAD_DIM = 128
ROPE_THETA = 10000.0
EPS = 1e-6

LANES = 128
BF16_SUBLANES = 16
VMEM_LIMIT = 56 * 1024 * 1024

FFT_R = 128
FFT_N = FFT_R * FFT_R
SEQ_LEN = FFT_N // 2
FFT_HALF = FFT_R // 2
HY_CBLK = 16


def _cparams(sem):
    return pltpu.CompilerParams(dimension_semantics=sem, vmem_limit_bytes=VMEM_LIMIT)


def _silu(g):
    return g * (1.0 / (1.0 + jnp.exp(-g)))


def _rms_norm_rows(x, g):
    ms = jnp.mean(x * x, axis=-1, keepdims=True)
    return x * lax.rsqrt(ms + EPS) * g


def _norm_matmul_kernel(x_ref, g_ref, w_ref, *refs, rope, q_scale):
    if rope:
        cos_ref, sin_ref, o_ref, h_ref = refs
    else:
        o_ref, h_ref = refs
    j = pl.program_id(1)

    @pl.when(j == 0)
    def _():
        h_ref[...] = _rms_norm_rows(x_ref[...], g_ref[...]).astype(BF16)

    acc = jnp.dot(h_ref[...], w_ref[...], preferred_element_type=F32)
    if rope:
        tm, tn = acc.shape
        scale = jnp.where(j == 0, q_scale, 1.0).astype(F32)
        cos = cos_ref[...]
        sin = sin_ref[...]
        lane = lax.broadcasted_iota(jnp.int32, (tm, LANES), 1)
        first_half = (lane % DA_QK_DIM) < (DA_QK_DIM // 2)
        for c in range(tn // LANES):
            a = acc[:, c * LANES:(c + 1) * LANES]
            partner = jnp.where(first_half,
                                pltpu.roll(a, LANES - DA_QK_DIM // 2, 1),
                                pltpu.roll(a, DA_QK_DIM // 2, 1))
            o_ref[:, c * LANES:(c + 1) * LANES] = ((a * cos + partner * sin) * scale).astype(o_ref.dtype)
    else:
        o_ref[...] = acc.astype(o_ref.dtype)


def _norm_matmul(x, g, w, out_dtype, name, rope_tabs=None, q_scale=1.0, tm=1024, tn=512):
    m, d = x.shape
    n = w.shape[1]
    tm = min(tm, m)
    assert m % tm == 0 and n % tn == 0
    in_specs = [
        pl.BlockSpec((tm, d), lambda i, j: (i, 0)),
        pl.BlockSpec((1, d), lambda i, j: (0, 0)),
        pl.BlockSpec((d, tn), lambda i, j: (0, j)),
    ]
    args = [x, g.reshape(1, d), w]
    if rope_tabs is not None:
        n_pos = rope_tabs[0].shape[0] // tm
        for t in rope_tabs:
            in_specs.append(pl.BlockSpec((tm, LANES), lambda i, j: (i % n_pos, 0)))
            args.append(t)
    return pl.pallas_call(
        functools.partial(_norm_matmul_kernel, rope=rope_tabs is not None, q_scale=q_scale),
        grid=(m // tm, n // tn),
        in_specs=in_specs,
        out_specs=pl.BlockSpec((tm, tn), lambda i, j: (i, j)),
        out_shape=jax.ShapeDtypeStruct((m, n), out_dtype),
        scratch_shapes=[pltpu.VMEM((tm, d), BF16)],
        compiler_params=_cparams(("parallel", "arbitrary")),
        name=name,
    )(*args)


PROJ_TN = 1024
PROJ_TOKEN_TILES = 3
PROJ_CHANNEL_TILES = 2


def _input_proj_kernel(x_ref, g_ref, w_ref, wt_ref, cos_ref, sin_ref, qk_ref, vm_ref, gate_ref, hy_ref, h_ref,
                       *, q_scale):
    j = pl.program_id(1)
    tn = PROJ_TN

    @pl.when(j == 0)
    def _():
        h_ref[...] = _rms_norm_rows(x_ref[...], g_ref[...]).astype(BF16)

    def token_tile(t):
        return jnp.dot(h_ref[...], w_ref[:, t * tn:(t + 1) * tn], preferred_element_type=F32)

    @pl.when(j == 0)
    def _():
        acc = token_tile(0)
        tm = acc.shape[0]
        cos = cos_ref[...]
        sin = sin_ref[...]
        lane = lax.broadcasted_iota(jnp.int32, (tm, LANES), 1)
        first_half = (lane % DA_QK_DIM) < (DA_QK_DIM // 2)
        for c in range(tn // LANES):
            scale = q_scale if c * LANES < GROUP_W else 1.0
            a = acc[:, c * LANES:(c + 1) * LANES]
            partner = jnp.where(first_half,
                                pltpu.roll(a, LANES - DA_QK_DIM // 2, 1),
                                pltpu.roll(a, DA_QK_DIM // 2, 1))
            qk_ref[:, c * LANES:(c + 1) * LANES] = ((a * cos + partner * sin) * scale).astype(qk_ref.dtype)

    @pl.when(j == 1)
    def _():
        vm_ref[...] = token_tile(1).astype(vm_ref.dtype)

    @pl.when(j == 2)
    def _():
        gate_ref[...] = token_tile(2)

    for t in range(PROJ_CHANNEL_TILES):
        @pl.when(j == PROJ_TOKEN_TILES + t)
        def _(t=t):
            hy_ref[0] = lax.dot_general(wt_ref[t * tn:(t + 1) * tn, :], h_ref[...], (((1,), (1,)), ((), ())),
                                        preferred_element_type=F32)


def _input_proj(x, g, w_tok, w_ch_t, rope_tabs, q_scale, batch, seq, tm=1024):
    m, d = x.shape
    tn = PROJ_TN
    assert w_tok.shape == (d, PROJ_TOKEN_TILES * tn) and w_ch_t.shape == (PROJ_CHANNEL_TILES * tn, d)
    per_batch = seq // tm
    once = pl.Buffered(1)
    clip = lambda j, lo, n: jnp.clip(j - lo, 0, n - 1)
    return pl.pallas_call(
        functools.partial(_input_proj_kernel, q_scale=q_scale),
        grid=(m // tm, PROJ_TOKEN_TILES + PROJ_CHANNEL_TILES),
        in_specs=[
            pl.BlockSpec((tm, d), lambda i, j: (i, 0)),
            pl.BlockSpec((1, d), lambda i, j: (0, 0)),
            pl.BlockSpec(w_tok.shape, lambda i, j: (0, 0), pipeline_mode=once),
            pl.BlockSpec(w_ch_t.shape, lambda i, j: (0, 0), pipeline_mode=once),
            pl.BlockSpec((tm, LANES), lambda i, j: (i % per_batch, 0)),
            pl.BlockSpec((tm, LANES), lambda i, j: (i % per_batch, 0)),
        ],
        out_specs=[
            pl.BlockSpec((tm, tn), lambda i, j: (i, 0)),
            pl.BlockSpec((tm, tn), lambda i, j: (i, 0)),
            pl.BlockSpec((tm, tn), lambda i, j: (i, 0)),
            pl.BlockSpec((1, tn, tm), lambda i, j: (i // per_batch, clip(j, PROJ_TOKEN_TILES, PROJ_CHANNEL_TILES),
                                                    i % per_batch)),
        ],
        out_shape=[
            jax.ShapeDtypeStruct((m, tn), BF16),
            jax.ShapeDtypeStruct((m, tn), BF16),
            jax.ShapeDtypeStruct((m, tn), F32),
            jax.ShapeDtypeStruct((batch, PROJ_CHANNEL_TILES * tn, seq), F32),
        ],
        scratch_shapes=[pltpu.VMEM((tm, d), BF16)],
        compiler_params=_cparams(("parallel", "arbitrary")),
        name="input_proj",
    )(x, g.reshape(1, d), w_tok, w_ch_t, rope_tabs[0], rope_tabs[1])


def _diff_attn_kernel(q_ref, k_ref, v_ref, gate_ref, gsub_ref, lq1_ref, lk1_ref, lq2_ref, lk2_ref,
                      o_ref, vt_ref, m_ref, acc_ref, s_a, s_b, cm_a, cm_b, p_a, p_b,
                      *, lam_init, tq, tk, t_chunk):
    seq = k_ref.shape[0]
    dv = DA_V_DIM
    assert (seq // tk) % 2 == 0 and seq // tk >= 2

    @pl.when(pl.program_id(2) == 0)
    def _():
        row = lax.broadcasted_iota(jnp.int32, (BF16_SUBLANES, t_chunk), 0)
        ones_row = jnp.where(row == 0, 1.0, 0.0).astype(BF16)
        for c in range(seq // t_chunk):
            blk = v_ref[c * t_chunk:(c + 1) * t_chunk, :].astype(F32)
            vt_ref[0:dv, c * t_chunk:(c + 1) * t_chunk] = blk.T.astype(BF16)
            vt_ref[dv:dv + BF16_SUBLANES, c * t_chunk:(c + 1) * t_chunk] = ones_row

    q = q_ref[...]
    lane = lax.broadcasted_iota(jnp.int32, q.shape, 1)
    zero = jnp.zeros_like(q)
    q2 = jnp.concatenate([jnp.where(lane < DA_QK_DIM, q, zero),
                          jnp.where(lane >= DA_QK_DIM, q, zero)], axis=0)

    m_ref[...] = jnp.full(m_ref.shape, -1e30, F32)
    acc_ref[...] = jnp.zeros(acc_ref.shape, F32)

    def scores(c, s_ref, cm_ref):
        kc = k_ref[pl.ds(pl.multiple_of(c * tk, tk), tk), :]
        s = lax.dot_general(kc, q2, (((1,), (1,)), ((), ())), preferred_element_type=F32)
        s_ref[...] = s
        cm_ref[...] = jnp.max(s, axis=0, keepdims=True)

    def weighted_values(c, p_ref):
        vtc = vt_ref[:, pl.ds(pl.multiple_of(c * tk, tk), tk)]
        return jnp.dot(vtc, p_ref[...], preferred_element_type=F32)

    def softmax_chunk(s_ref, cm_ref, p_ref):
        m_prev = m_ref[...]
        m_new = jnp.maximum(m_prev, cm_ref[...])
        p_ref[...] = jnp.exp2(s_ref[...] - m_new).astype(BF16)
        m_ref[...] = m_new
        return jnp.exp2(m_prev - m_new)

    def step(c, s_cur, cm_cur, s_next, cm_next, p_cur, p_prev):
        scores(c + 1, s_next, cm_next)
        pending = weighted_values(c - 1, p_prev)
        alpha = softmax_chunk(s_cur, cm_cur, p_cur)
        acc_ref[...] = (acc_ref[...] + pending) * alpha

    n_chunks = seq // tk
    scores(0, s_a, cm_a)
    scores(1, s_b, cm_b)
    softmax_chunk(s_a, cm_a, p_a)

    def pair(j, carry):
        c = 1 + 2 * j
        step(c, s_b, cm_b, s_a, cm_a, p_b, p_a)
        step(c + 1, s_a, cm_a, s_b, cm_b, p_a, p_b)
        return carry
    lax.fori_loop(0, (n_chunks - 2) // 2, pair, 0)

    last = n_chunks - 1
    pending = weighted_values(last - 1, p_a)
    alpha = softmax_chunk(s_b, cm_b, p_b)
    acc = (acc_ref[...] + pending) * alpha + weighted_values(last, p_b)

    lam = (jnp.exp(jnp.sum(lq1_ref[...] * lk1_ref[...], axis=-1, keepdims=True))
           - jnp.exp(jnp.sum(lq2_ref[...] * lk2_ref[...], axis=-1, keepdims=True)) + lam_init)
    on = acc[0:dv] / acc[dv:dv + 1]
    ot = on[:, :tq] - lam * on[:, tq:]
    o = ot.T
    y = _rms_norm_rows(o, gsub_ref[...]) * (1.0 - lam_init)
    o_ref[...] = (y * _silu(gate_ref[...])).astype(o_ref.dtype)


def _diff_attention(qk, vm, gates, g_sub, lq1, lk1, lq2, lk2, lam_init, batch, seq, tq=2048, tk=512):
    nq = seq // tq
    head_blocks = GROUP_W // LANES
    vec = lambda a: a.reshape(1, -1).astype(F32)
    small = lambda n: pl.BlockSpec((1, n), lambda b, h, i: (0, 0))
    return pl.pallas_call(
        functools.partial(_diff_attn_kernel, lam_init=lam_init, tq=tq, tk=tk, t_chunk=512),
        grid=(batch, DA_HEADS, nq),
        in_specs=[
            pl.BlockSpec((tq, LANES), lambda b, h, i: (b * nq + i, h)),
            pl.BlockSpec((seq, LANES), lambda b, h, i: (b, head_blocks + h)),
            pl.BlockSpec((seq, LANES), lambda b, h, i: (b, h)),
            pl.BlockSpec((tq, LANES), lambda b, h, i: (b * nq + i, h)),
            small(DA_V_DIM), small(DA_QK_DIM), small(DA_QK_DIM), small(DA_QK_DIM), small(DA_QK_DIM),
        ],
        out_specs=pl.BlockSpec((tq, LANES), lambda b, h, i: (b * nq + i, h)),
        out_shape=jax.ShapeDtypeStruct((batch * seq, GROUP_W), BF16),
        scratch_shapes=[
            pltpu.VMEM((DA_V_DIM + BF16_SUBLANES, seq), BF16),
            pltpu.VMEM((1, 2 * tq), F32),
            pltpu.VMEM((DA_V_DIM + BF16_SUBLANES, 2 * tq), F32),
            pltpu.VMEM((tk, 2 * tq), F32), pltpu.VMEM((tk, 2 * tq), F32),
            pltpu.VMEM((1, 2 * tq), F32), pltpu.VMEM((1, 2 * tq), F32),
            pltpu.VMEM((tk, 2 * tq), BF16), pltpu.VMEM((tk, 2 * tq), BF16),
        ],
        compiler_params=_cparams(("parallel", "parallel", "arbitrary")),
        name="diff_attention",
    )(qk, qk, vm, gates, vec(g_sub), vec(lq1), vec(lk1), vec(lq2), vec(lk2))


def _filter_kernel(z_ref, zs_ref, w1_ref, b1_ref, w2_ref, b2_ref, w3_ref, b3_ref, w4f_ref, w4b_ref,
                   fr_ref, df_ref, db_ref, o_ref):
    hp = lax.Precision.HIGHEST
    fr = fr_ref[...]

    def mlp(z):
        h = jnp.sin(fr * (jnp.dot(w1_ref[...], z, precision=hp, preferred_element_type=F32) + b1_ref[...]))
        h = jnp.sin(fr * (jnp.dot(w2_ref[...], h, precision=hp, preferred_element_type=F32) + b2_ref[...]))
        return jnp.sin(fr * (jnp.dot(w3_ref[...], h, precision=hp, preferred_element_type=F32) + b3_ref[...]))

    def dot_split(a, b):
        a_hi = a.astype(BF16)
        a_lo = (a - a_hi.astype(F32)).astype(BF16)
        b_hi = b.astype(BF16)
        b_lo = (b - b_hi.astype(F32)).astype(BF16)
        return (jnp.dot(a_hi, b_hi, preferred_element_type=F32)
                + (jnp.dot(a_hi, b_lo, preferred_element_type=F32)
                   + jnp.dot(a_lo, b_hi, preferred_element_type=F32)))

    z = z_ref[...]
    zs = zs_ref[...]
    tl = z.shape[1]
    ff = dot_split(w4f_ref[...], mlp(z))
    fb = dot_split(w4b_ref[...], mlp(zs))
    col = pl.program_id(0) * tl + lax.broadcasted_iota(jnp.int32, (1, tl), 1)
    dec_f = jnp.exp(-df_ref[...] * z[0:1, :])
    dec_b = jnp.where(col > 0, jnp.exp(-db_ref[...] * zs[0:1, :]), 0.0)
    c = HY_CH
    o_ref[0 * c:1 * c, :] = ff[:c] * dec_f
    o_ref[1 * c:2 * c, :] = fb[:c] * dec_b
    o_ref[2 * c:3 * c, :] = ff[c:] * dec_f
    o_ref[3 * c:4 * c, :] = fb[c:] * dec_b


def _hyena_filters(zt, zst, deltas_f, deltas_b, w1, b1, w2, b2, w3, b3, w4, freq, tl=512):
    seq = zt.shape[1]
    k = HF_ORDER
    w1t = jnp.zeros((k, k), F32).at[:, :HF_EMB].set(w1.astype(F32).T)
    w4r = w4.astype(F32).reshape(k, 2, 2, HY_CH)
    w4ft = w4r[:, :, 0, :].reshape(k, 2 * HY_CH).T
    w4bt = w4r[:, :, 1, :].reshape(k, 2 * HY_CH).T
    colv = lambda a: a.reshape(-1, 1).astype(F32)
    full = lambda shape: pl.BlockSpec(shape, lambda i: (0, 0))
    return pl.pallas_call(
        _filter_kernel,
        grid=(seq // tl,),
        in_specs=[
            pl.BlockSpec((k, tl), lambda i: (0, i)), pl.BlockSpec((k, tl), lambda i: (0, i)),
            full((k, k)), full((k, 1)), full((k, k)), full((k, 1)), full((k, k)), full((k, 1)),
            full((2 * HY_CH, k)), full((2 * HY_CH, k)), full((k, 1)), full((HY_CH, 1)), full((HY_CH, 1)),
        ],
        out_specs=pl.BlockSpec((4 * HY_CH, tl), lambda i: (0, i)),
        out_shape=jax.ShapeDtypeStruct((4 * HY_CH, seq), F32),
        compiler_params=_cparams(("parallel",)),
        name="hyena_filters",
    )(zt, zst, w1t, colv(b1), w2.astype(F32).T, colv(b2), w3.astype(F32).T, colv(b3), w4ft, w4bt,
      colv(freq), deltas_f, deltas_b)


def _dft_tables():
    k1 = np.arange(FFT_R, dtype=np.int64)[:, None]
    n1 = np.arange(FFT_HALF, dtype=np.int64)[None, :]
    a1 = -2.0 * np.pi * ((k1 * n1) % FFT_R) / FFT_R
    f1 = np.concatenate([np.cos(a1), np.sin(a1)], axis=0)
    f1_inv = np.concatenate([np.cos(a1).T, np.sin(a1).T], axis=1)
    n2 = np.arange(FFT_R, dtype=np.int64)[None, :]
    at = -2.0 * np.pi * ((k1 * n2) % FFT_N) / FFT_N
    k2 = np.arange(FFT_R, dtype=np.int64)[None, :]
    a2 = -2.0 * np.pi * ((n2.T * k2) % FFT_R) / FFT_R
    fr, fi = np.cos(a2), np.sin(a2)
    c2 = np.block([[fr, fi], [-fi, fr]])
    c2_inv = np.block([[fr, -fi], [fi, fr]])
    as_bf16 = lambda a: jnp.asarray(a, F32).astype(BF16)
    return dict(f1=as_bf16(f1), f1_inv=as_bf16(f1_inv), c2=as_bf16(c2), c2_inv=as_bf16(c2_inv),
                tr=jnp.asarray(np.cos(at), F32), ti=jnp.asarray(np.sin(at), F32))


def _dft_many(xs, f1_ref, c2_ref, tr, ti):
    r = FFT_R
    xx = jnp.concatenate([x.astype(BF16) for x in xs], axis=1)
    a = jnp.dot(f1_ref[...], xx, preferred_element_type=F32)
    rows = []
    for s in range(len(xs)):
        ar, ai = a[:r, s * r:(s + 1) * r], a[r:, s * r:(s + 1) * r]
        rows.append(jnp.concatenate([ar * tr - ai * ti, ar * ti + ai * tr], axis=1).astype(BF16))
    return jnp.dot(jnp.concatenate(rows, axis=0), c2_ref[...], preferred_element_type=F32)


def _idft_many(ys, f1i_ref, c2i_ref, tr, ti):
    r = FFT_R
    y = jnp.concatenate([v.astype(BF16) for v in ys], axis=0)
    b = jnp.dot(y, c2i_ref[...], preferred_element_type=F32)
    outs_r, outs_i = [], []
    for s in range(len(ys)):
        br, bi = b[s * r:(s + 1) * r, :r], b[s * r:(s + 1) * r, r:]
        outs_r.append((br * tr + bi * ti).astype(BF16))
        outs_i.append((bi * tr - br * ti).astype(BF16))
    rhs = jnp.concatenate([jnp.concatenate(outs_r, axis=1), jnp.concatenate(outs_i, axis=1)], axis=0)
    return jnp.dot(f1i_ref[...], rhs, preferred_element_type=F32)


def _filter_spectrum_kernel(hf_ref, hb_ref, f1_ref, c2_ref, tr_ref, ti_ref, o_ref):
    r = FFT_R
    cb = hf_ref.shape[2]
    xs = [ref[0, 0, ci] for ci in range(cb) for ref in (hf_ref, hb_ref)]
    x = _dft_many(xs, f1_ref, c2_ref, tr_ref[...], ti_ref[...])
    for ci in range(cb):
        xf, xb = x[2 * ci * r:(2 * ci + 1) * r], x[(2 * ci + 1) * r:(2 * ci + 2) * r]
        o_ref[0, ci] = jnp.concatenate([xf[:, :r] + xb[:, :r], xf[:, r:] - xb[:, r:]],
                                       axis=1).astype(o_ref.dtype)


def _filter_spectrum(filt5, tabs):
    n_ord, _, n_ch, rows, _ = filt5.shape
    cb = HY_CBLK
    const = lambda shape: pl.BlockSpec(shape, lambda o, c: (0, 0))
    return pl.pallas_call(
        _filter_spectrum_kernel,
        grid=(n_ord, n_ch // cb),
        in_specs=[
            pl.BlockSpec((1, 1, cb, rows, LANES), lambda o, c: (o, 0, c, 0, 0)),
            pl.BlockSpec((1, 1, cb, rows, LANES), lambda o, c: (o, 1, c, 0, 0)),
            const((2 * FFT_R, FFT_HALF)), const((2 * FFT_R, 2 * FFT_R)),
            const((FFT_R, FFT_R)), const((FFT_R, FFT_R)),
        ],
        out_specs=pl.BlockSpec((1, cb, FFT_R, 2 * FFT_R), lambda o, c: (o, c, 0, 0)),
        out_shape=jax.ShapeDtypeStruct((n_ord, n_ch, FFT_R, 2 * FFT_R), BF16),
        compiler_params=_cparams(("parallel", "parallel")),
        name="filter_spectrum",
    )(filt5, filt5, tabs["f1"], tabs["c2"], tabs["tr"], tabs["ti"])


def _short_conv_mat(u, w0, w1, w2, bias):
    rows, lanes = u.shape
    lane = lax.broadcasted_iota(jnp.int32, u.shape, 1)
    row = lax.broadcasted_iota(jnp.int32, u.shape, 0)
    prev = pltpu.roll(u, 1, 1)
    prev = jnp.where(lane == 0, jnp.where(row == 0, 0.0, pltpu.roll(prev, 1, 0)), prev)
    nxt = pltpu.roll(u, lanes - 1, 1)
    nxt = jnp.where(lane == lanes - 1, jnp.where(row == rows - 1, 0.0, pltpu.roll(nxt, rows - 1, 0)), nxt)
    return w0 * prev + w1 * u + w2 * nxt + bias


def _hyena_conv_kernel(cw_ref, d_ref, u_ref, gate_ref, *refs, conv_in, gated_out, u_ch0, gate_ch0, order_ch0):
    if gated_out:
        g2_ref, k_ref, f1_ref, f1i_ref, c2_ref, c2i_ref, tr_ref, ti_ref, o_ref = refs
    else:
        k_ref, f1_ref, f1i_ref, c2_ref, c2i_ref, tr_ref, ti_ref, o_ref = refs
        g2_ref = None
    r = FFT_R
    tr, ti = tr_ref[...], ti_ref[...]
    batch, cb = u_ref.shape[0], u_ref.shape[1]
    c0 = pl.program_id(0) * cb
    signals = [(ci, b) for ci in range(cb) for b in range(batch)]

    def conv_taps(ref, b, ci, ch):
        return _short_conv_mat(ref[b, ci], cw_ref[0, ch], cw_ref[1, ch], cw_ref[2, ch], cw_ref[3, ch])

    def long_conv_input(ci, b):
        return conv_taps(u_ref, b, ci, u_ch0 + c0 + ci) if conv_in else u_ref[b, ci]

    x = _dft_many([long_conv_input(ci, b) for ci, b in signals], f1_ref, c2_ref, tr, ti)
    ys = []
    for s, (ci, b) in enumerate(signals):
        kk = k_ref[0, ci].astype(F32)
        kr, ki = kk[:, :r], kk[:, r:]
        xr, xi = x[s * r:(s + 1) * r, :r], x[s * r:(s + 1) * r, r:]
        ys.append(jnp.concatenate([xr * kr - xi * ki, xr * ki + xi * kr], axis=1))
    y = _idft_many(ys, f1i_ref, c2i_ref, tr, ti)
    for s, (ci, b) in enumerate(signals):
        gate = conv_taps(gate_ref, b, ci, gate_ch0 + c0 + ci)
        skip = d_ref[0, order_ch0 + c0 + ci] * long_conv_input(ci, b)
        res = gate * (y[:, s * r:(s + 1) * r] * (1.0 / FFT_N) + skip)
        if gated_out:
            res = res * _silu(g2_ref[b, ci])
        o_ref[b, ci] = res.astype(o_ref.dtype)


def _hyena_conv(conv_wb, d_skip, u_arr, u_ch0, gate_arr, gate_ch0, g2, spec, order, tabs, out_dtype, conv_in):
    batch = u_arr.shape[0]
    assert batch == 2 and u_arr.shape[2] == FFT_HALF
    cb = HY_CBLK
    smem = pl.BlockSpec(memory_space=pltpu.SMEM)
    chan = lambda ch0: pl.BlockSpec((batch, cb, FFT_HALF, LANES), lambda c: (0, ch0 // cb + c, 0, 0))
    const = lambda shape: pl.BlockSpec(shape, lambda c: (0, 0))
    in_specs = [smem, smem, chan(u_ch0), chan(gate_ch0)]
    args = [conv_wb, d_skip, u_arr, gate_arr]
    if g2 is not None:
        in_specs.append(chan(g2[1]))
        args.append(g2[0])
    in_specs += [
        pl.BlockSpec((1, cb, FFT_R, 2 * FFT_R), lambda c: (order, c, 0, 0)),
        const((2 * FFT_R, FFT_HALF)), const((FFT_HALF, 2 * FFT_R)),
        const((2 * FFT_R, 2 * FFT_R)), const((2 * FFT_R, 2 * FFT_R)),
        const((FFT_R, FFT_R)), const((FFT_R, FFT_R)),
    ]
    args += [spec, tabs["f1"], tabs["f1_inv"], tabs["c2"], tabs["c2_inv"], tabs["tr"], tabs["ti"]]
    return pl.pallas_call(
        functools.partial(_hyena_conv_kernel, conv_in=conv_in, gated_out=g2 is not None,
                          u_ch0=u_ch0, gate_ch0=gate_ch0, order_ch0=order * HY_CH),
        grid=(HY_CH // cb,),
        in_specs=in_specs,
        out_specs=pl.BlockSpec((batch, cb, FFT_HALF, LANES), lambda c: (0, c, 0, 0)),
        out_shape=jax.ShapeDtypeStruct((batch, HY_CH, FFT_HALF, LANES), out_dtype),
        compiler_params=_cparams(("parallel",)),
        name="hyena_conv_gated" if g2 is not None else "hyena_conv",
    )(*args)


def _out_kernel(ya_ref, ybt_ref, mq_ref, mg_ref, mk_ref, mv_ref, w_ref, x_ref, gf_ref, o_ref, *, final):
    gw = GROUP_W
    acc = x_ref[...]
    acc += jnp.dot(ya_ref[...], w_ref[0:gw, :], preferred_element_type=F32)
    acc += lax.dot_general(ybt_ref[0], w_ref[gw:2 * gw, :], (((0,), (0,)), ((), ())),
                           preferred_element_type=F32)
    hd = MEM_HEAD_DIM
    heads = []
    for h in range(MEM_HEADS):
        sl = slice(h * hd, (h + 1) * hd)
        s = lax.dot_general(mq_ref[:, sl], mk_ref[:, sl], (((1,), (1,)), ((), ())),
                            preferred_element_type=F32) * (hd ** -0.5)
        e = jnp.exp(s - jnp.max(s, axis=-1, keepdims=True))
        p = e / jnp.sum(e, axis=-1, keepdims=True)
        oh = jnp.dot(p.astype(BF16), mv_ref[:, sl], preferred_element_type=F32)
        heads.append((oh * _silu(mg_ref[:, sl])).astype(BF16))
    acc += jnp.dot(jnp.concatenate(heads, axis=1), w_ref[2 * gw:3 * gw, :], preferred_element_type=F32)
    if final:
        acc = _rms_norm_rows(acc, gf_ref[...])
    o_ref[...] = acc


def _out_proj(ya, ybt, vm, feats, mg_blk, mkv, w_out, x, g_final, final, seq, mem_len, tm=512):
    m, d = x.shape
    gw = GROUP_W
    per_batch = seq // tm
    return pl.pallas_call(
        functools.partial(_out_kernel, final=final),
        grid=(m // tm,),
        in_specs=[
            pl.BlockSpec((tm, gw), lambda i: (i, 0)),
            pl.BlockSpec((1, gw, tm), lambda i: (i // per_batch, 0, i % per_batch)),
            pl.BlockSpec((tm, gw), lambda i: (i, 1)),
            pl.BlockSpec((tm, gw), lambda i: (i, mg_blk)),
            pl.BlockSpec((mem_len, gw), lambda i: (i // per_batch, 0)),
            pl.BlockSpec((mem_len, gw), lambda i: (i // per_batch, 1)),
            pl.BlockSpec((3 * gw, d), lambda i: (0, 0)),
            pl.BlockSpec((tm, d), lambda i: (i, 0)),
            pl.BlockSpec((1, d), lambda i: (0, 0)),
        ],
        out_specs=pl.BlockSpec((tm, d), lambda i: (i, 0)),
        out_shape=jax.ShapeDtypeStruct((m, d), F32),
        compiler_params=_cparams(("parallel",)),
        name="out_proj_final" if final else "out_proj",
    )(ya, ybt, vm, feats, mkv, mkv, w_out, x, g_final.reshape(1, d).astype(F32))


def _rope_tables(seq):
    pos = jnp.arange(seq, dtype=F32)
    inv_freq = ROPE_THETA ** (-jnp.arange(0, DA_QK_DIM, 2, dtype=F32) / DA_QK_DIM)
    ang = pos[:, None] * inv_freq[None, :]
    cos, sin = jnp.cos(ang), jnp.sin(ang)
    reps = LANES // DA_QK_DIM
    return (jnp.tile(jnp.concatenate([cos, cos], axis=-1), (1, reps)),
            jnp.tile(jnp.concatenate([-sin, sin], axis=-1), (1, reps)))


def _filter_features(seq):
    t = jnp.linspace(0.0, 1.0, seq, dtype=F32)[:, None]
    w = 2.0 * math.pi * jnp.arange(seq, dtype=F32)[:, None] / seq
    f = jnp.linspace(1e-4, HF_BANDS - 1, HF_BANDS, dtype=F32)[None, :]
    z = jnp.concatenate([t, jnp.cos(f * w), -jnp.sin(f * w)], axis=-1)
    z = jnp.pad(z, ((0, 0), (0, HF_ORDER - HF_EMB)))
    zs = jnp.concatenate([z[:1], z[:-1]], axis=0)
    max_decay = math.log(HF_TARGET) / HF_FAST
    min_decay = math.log(HF_TARGET) / HF_SLOW
    deltas = jnp.abs(jnp.linspace(min_decay, max_decay, HY_CH, dtype=F32))
    return z.T, zs.T, deltas[:, None], deltas[::-1][:, None]


def kernel(x, mem, g_norm, w_in, da_lam_q1, da_lam_k1, da_lam_q2, da_lam_k2, da_subln_g, hy_conv_w, hy_conv_b, hf_w1, hf_b1, hf_w2, hf_b2, hf_w3, hf_b3, hf_w4, hf_freq, hy_skip, g_mem, w_mem_kv, w_out, g_final):
    batch, seq, d = x.shape
    assert seq == SEQ_LEN
    mem_len = mem.shape[1]
    depth = w_in.shape[0]
    gw = GROUP_W
    xf = x.reshape(batch * seq, d).astype(F32)
    memf = mem.reshape(batch * mem_len, d).astype(F32)
    rope_tabs = _rope_tables(seq)
    zt, zst, deltas_f, deltas_b = _filter_features(seq)
    tabs = _dft_tables()

    for l in range(depth):
        w = w_in[l].astype(BF16)
        w_tok = jnp.concatenate([w[:, :3 * gw], w[:, 8 * gw:9 * gw], w[:, 3 * gw:4 * gw], w[:, 9 * gw:]], axis=1)
        w_hy_t = w[:, 4 * gw:8 * gw].T
        qk, vm, gates, hy = _input_proj(xf, g_norm[l], w_tok, w_hy_t, rope_tabs,
                                        DA_QK_DIM ** -0.5 * math.log2(math.e),
                                        batch, seq)
        hy = hy.reshape(batch, 4 * HY_CH, FFT_HALF, LANES)
        mkv = _norm_matmul(memf, g_mem[l], w_mem_kv[l].astype(BF16), BF16, "proj_mem_kv")

        lam_init = 0.8 - 0.6 * math.exp(-0.3 * l)
        ya = _diff_attention(qk, vm, gates, da_subln_g[l], da_lam_q1[l], da_lam_k1[l],
                             da_lam_q2[l], da_lam_k2[l], lam_init, batch, seq)

        filt = _hyena_filters(zt, zst, deltas_f, deltas_b, hf_w1[l], hf_b1[l], hf_w2[l], hf_b2[l],
                              hf_w3[l], hf_b3[l], hf_w4[l], hf_freq[l])
        spec = _filter_spectrum(filt.reshape(2, 2, HY_CH, FFT_HALF, LANES), tabs)
        conv_wb = jnp.concatenate([hy_conv_w[l], hy_conv_b[l][None, :]], axis=0).astype(F32)
        d_skip = hy_skip[l].reshape(1, -1).astype(F32)
        z1 = _hyena_conv(conv_wb, d_skip, hy, 0, hy, HY_CH, None, spec, 0, tabs, F32, conv_in=True)
        ybt = _hyena_conv(conv_wb, d_skip, z1, 0, hy, 2 * HY_CH, (hy, 3 * HY_CH), spec, 1, tabs, BF16,
                          conv_in=False)
        ybt = ybt.reshape(batch, HY_CH, seq)

        xf = _out_proj(ya, ybt, vm, gates, 1, mkv, w_out[l].astype(BF16), xf, g_final,
                       l == depth - 1, seq, mem_len)
    return xf.reshape(batch, seq, d)
```

```python
import functools
import math

import jax
import jax.numpy as jnp
import numpy as np
from jax import lax
from jax.experimental import pallas as pl
from jax.experimental.pallas import tpu as pltpu

F32 = jnp.float32
BF16 = jnp.bfloat16

GROUP_W = 512
DA_HEADS = 4
DA_QK_DIM = 64
DA_V_DIM = 128
HY_CH = 512
HF_EMB = 33
HF_BANDS = 16
HF_ORDER = 64
HF_TARGET = 1e-2
HF_FAST = 0.3
HF_SLOW = 1.5
MEM_HEADS = 4
MEM_HEAD_DIM = 128
ROPE_THETA = 10000.0
EPS = 1e-6

LANES = 128
BF16_SUBLANES = 16
VMEM_LIMIT = 56 * 1024 * 1024

FFT_R = 128
FFT_N = FFT_R * FFT_R
SEQ_LEN = FFT_N // 2
FFT_HALF = FFT_R // 2
HY_CBLK = 16


def _cparams(sem):
    return pltpu.CompilerParams(dimension_semantics=sem, vmem_limit_bytes=VMEM_LIMIT)


def _silu(g):
    return g * (1.0 / (1.0 + jnp.exp(-g)))


def _rms_norm_rows(x, g):
    ms = jnp.mean(x * x, axis=-1, keepdims=True)
    return x * lax.rsqrt(ms + EPS) * g


def _norm_matmul_kernel(x_ref, g_ref, w_ref, *refs, rope, q_scale):
    if rope:
        cos_ref, sin_ref, o_ref, h_ref = refs
    else:
        o_ref, h_ref = refs
    j = pl.program_id(1)

    @pl.when(j == 0)
    def _():
        h_ref[...] = _rms_norm_rows(x_ref[...], g_ref[...]).astype(BF16)

    acc = jnp.dot(h_ref[...], w_ref[...], preferred_element_type=F32)
    if rope:
        tm, tn = acc.shape
        scale = jnp.where(j == 0, q_scale, 1.0).astype(F32)
        cos = cos_ref[...]
        sin = sin_ref[...]
        lane = lax.broadcasted_iota(jnp.int32, (tm, LANES), 1)
        first_half = (lane % DA_QK_DIM) < (DA_QK_DIM // 2)
        for c in range(tn // LANES):
            a = acc[:, c * LANES:(c + 1) * LANES]
            partner = jnp.where(first_half,
                                pltpu.roll(a, LANES - DA_QK_DIM // 2, 1),
                                pltpu.roll(a, DA_QK_DIM // 2, 1))
            o_ref[:, c * LANES:(c + 1) * LANES] = ((a * cos + partner * sin) * scale).astype(o_ref.dtype)
    else:
        o_ref[...] = acc.astype(o_ref.dtype)


def _norm_matmul(x, g, w, out_dtype, name, rope_tabs=None, q_scale=1.0, tm=1024, tn=512):
    m, d = x.shape
    n = w.shape[1]
    tm = min(tm, m)
    assert m % tm == 0 and n % tn == 0
    in_specs = [
        pl.BlockSpec((tm, d), lambda i, j: (i, 0)),
        pl.BlockSpec((1, d), lambda i, j: (0, 0)),
        pl.BlockSpec((d, tn), lambda i, j: (0, j)),
    ]
    args = [x, g.reshape(1, d), w]
    if rope_tabs is not None:
        n_pos = rope_tabs[0].shape[0] // tm
        for t in rope_tabs:
            in_specs.append(pl.BlockSpec((tm, LANES), lambda i, j: (i % n_pos, 0)))
            args.append(t)
    return pl.pallas_call(
        functools.partial(_norm_matmul_kernel, rope=rope_tabs is not None, q_scale=q_scale),
        grid=(m // tm, n // tn),
        in_specs=in_specs,
        out_specs=pl.BlockSpec((tm, tn), lambda i, j: (i, j)),
        out_shape=jax.ShapeDtypeStruct((m, n), out_dtype),
        scratch_shapes=[pltpu.VMEM((tm, d), BF16)],
        compiler_params=_cparams(("parallel", "arbitrary")),
        name=name,
    )(*args)


PROJ_TN = 1024
PROJ_TOKEN_TILES = 3
PROJ_CHANNEL_TILES = 2


def _input_proj_kernel(x_ref, g_ref, w_ref, wt_ref, cos_ref, sin_ref, qk_ref, vm_ref, gate_ref, hy_ref, h_ref,
                       *, q_scale):
    j = pl.program_id(1)
    tn = PROJ_TN

    @pl.when(j == 0)
    def _():
        h_ref[...] = _rms_norm_rows(x_ref[...], g_ref[...]).astype(BF16)

    def token_tile(t):
        return jnp.dot(h_ref[...], w_ref[:, t * tn:(t + 1) * tn], preferred_element_type=F32)

    @pl.when(j == 0)
    def _():
        acc = token_tile(0)
        tm = acc.shape[0]
        cos = cos_ref[...]
        sin = sin_ref[...]
        lane = lax.broadcasted_iota(jnp.int32, (tm, LANES), 1)
        first_half = (lane % DA_QK_DIM) < (DA_QK_DIM // 2)
        for c in range(tn // LANES):
            scale = q_scale if c * LANES < GROUP_W else 1.0
            a = acc[:, c * LANES:(c + 1) * LANES]
            partner = jnp.where(first_half,
                                pltpu.roll(a, LANES - DA_QK_DIM // 2, 1),
                                pltpu.roll(a, DA_QK_DIM // 2, 1))
            qk_ref[:, c * LANES:(c + 1) * LANES] = ((a * cos + partner * sin) * scale).astype(qk_ref.dtype)

    @pl.when(j == 1)
    def _():
        vm_ref[...] = token_tile(1).astype(vm_ref.dtype)

    @pl.when(j == 2)
    def _():
        gate_ref[...] = token_tile(2)

    for t in range(PROJ_CHANNEL_TILES):
        @pl.when(j == PROJ_TOKEN_TILES + t)
        def _(t=t):
            hy_ref[0] = lax.dot_general(wt_ref[t * tn:(t + 1) * tn, :], h_ref[...], (((1,), (1,)), ((), ())),
                                        preferred_element_type=F32)


def _input_proj(x, g, w_tok, w_ch_t, rope_tabs, q_scale, batch, seq, tm=1024):
    m, d = x.shape
    tn = PROJ_TN
    assert w_tok.shape == (d, PROJ_TOKEN_TILES * tn) and w_ch_t.shape == (PROJ_CHANNEL_TILES * tn, d)
    per_batch = seq // tm
    once = pl.Buffered(1)
    clip = lambda j, lo, n: jnp.clip(j - lo, 0, n - 1)
    return pl.pallas_call(
        functools.partial(_input_proj_kernel, q_scale=q_scale),
        grid=(m // tm, PROJ_TOKEN_TILES + PROJ_CHANNEL_TILES),
        in_specs=[
            pl.BlockSpec((tm, d), lambda i, j: (i, 0)),
            pl.BlockSpec((1, d), lambda i, j: (0, 0)),
            pl.BlockSpec(w_tok.shape, lambda i, j: (0, 0), pipeline_mode=once),
            pl.BlockSpec(w_ch_t.shape, lambda i, j: (0, 0), pipeline_mode=once),
            pl.BlockSpec((tm, LANES), lambda i, j: (i % per_batch, 0)),
            pl.BlockSpec((tm, LANES), lambda i, j: (i % per_batch, 0)),
        ],
        out_specs=[
            pl.BlockSpec((tm, tn), lambda i, j: (i, 0)),
            pl.BlockSpec((tm, tn), lambda i, j: (i, 0)),
            pl.BlockSpec((tm, tn), lambda i, j: (i, 0)),
            pl.BlockSpec((1, tn, tm), lambda i, j: (i // per_batch, clip(j, PROJ_TOKEN_TILES, PROJ_CHANNEL_TILES),
                                                    i % per_batch)),
        ],
        out_shape=[
            jax.ShapeDtypeStruct((m, tn), BF16),
            jax.ShapeDtypeStruct((m, tn), BF16),
            jax.ShapeDtypeStruct((m, tn), F32),
            jax.ShapeDtypeStruct((batch, PROJ_CHANNEL_TILES * tn, seq), F32),
        ],
        scratch_shapes=[pltpu.VMEM((tm, d), BF16)],
        compiler_params=_cparams(("parallel", "arbitrary")),
        name="input_proj",
    )(x, g.reshape(1, d), w_tok, w_ch_t, rope_tabs[0], rope_tabs[1])


def _diff_attn_kernel(q_ref, k_ref, v_ref, gate_ref, gsub_ref, lq1_ref, lk1_ref, lq2_ref, lk2_ref,
                      o_ref, vt_ref, q2_ref, m_ref, acc_ref, accf_ref, s_a, s_b, cm_a, cm_b, p_a, p_b,
                      *, lam_init, tq, tk, t_chunk):
    seq = k_ref.shape[0]
    dv = DA_V_DIM
    n_tiles = seq // tq
    n_chunks = seq // tk
    assert n_chunks % 2 == 0 and n_chunks >= 4

    row = lax.broadcasted_iota(jnp.int32, (BF16_SUBLANES, t_chunk), 0)
    ones_row = jnp.where(row == 0, 1.0, 0.0).astype(BF16)
    for c in range(seq // t_chunk):
        blk = v_ref[c * t_chunk:(c + 1) * t_chunk, :].astype(F32)
        vt_ref[0:dv, c * t_chunk:(c + 1) * t_chunk] = blk.T.astype(BF16)
        vt_ref[dv:dv + BF16_SUBLANES, c * t_chunk:(c + 1) * t_chunk] = ones_row

    lam = (jnp.exp(jnp.sum(lq1_ref[...] * lk1_ref[...], axis=-1, keepdims=True))
           - jnp.exp(jnp.sum(lq2_ref[...] * lk2_ref[...], axis=-1, keepdims=True)) + lam_init)

    def tile_rows(i):
        return pl.ds(pl.multiple_of(i * tq, tq), tq)

    def load_queries(i):
        q = q_ref[tile_rows(i), :]
        lane = lax.broadcasted_iota(jnp.int32, q.shape, 1)
        zero = jnp.zeros_like(q)
        q2_ref[...] = jnp.concatenate([jnp.where(lane < DA_QK_DIM, q, zero),
                                       jnp.where(lane >= DA_QK_DIM, q, zero)], axis=0)

    def finish_tile(i):
        for g in range(tq // LANES):
            rows = pl.ds(pl.multiple_of(i * tq + g * LANES, LANES), LANES)
            c0 = slice(g * LANES, (g + 1) * LANES)
            c1 = slice(tq + g * LANES, tq + (g + 1) * LANES)
            ot = (accf_ref[0:dv, c0] / accf_ref[dv:dv + 1, c0]
                  - lam * (accf_ref[0:dv, c1] / accf_ref[dv:dv + 1, c1]))
            y = _rms_norm_rows(ot.T, gsub_ref[...]) * (1.0 - lam_init)
            o_ref[rows, :] = (y * _silu(gate_ref[rows, :])).astype(o_ref.dtype)

    def scores(c, s_ref, cm_ref):
        kc = k_ref[pl.ds(pl.multiple_of(c * tk, tk), tk), :]
        s = lax.dot_general(kc, q2_ref[...], (((1,), (1,)), ((), ())), preferred_element_type=F32)
        s_ref[...] = s
        cm_ref[...] = jnp.max(s, axis=0, keepdims=True)

    def weighted_values(c, p_ref):
        vtc = vt_ref[:, pl.ds(pl.multiple_of(c * tk, tk), tk)]
        return jnp.dot(vtc, p_ref[...], preferred_element_type=F32)

    def softmax_chunk(s_ref, cm_ref, p_ref):
        m_prev = m_ref[...]
        m_new = jnp.maximum(m_prev, cm_ref[...])
        p_ref[...] = jnp.exp2(s_ref[...] - m_new).astype(BF16)
        m_ref[...] = m_new
        return jnp.exp2(m_prev - m_new)

    def step(c, s_cur, cm_cur, s_next, cm_next, p_cur, p_prev):
        scores(c + 1, s_next, cm_next)
        pending = weighted_values(c - 1, p_prev)
        alpha = softmax_chunk(s_cur, cm_cur, p_cur)
        acc_ref[...] = (acc_ref[...] + pending) * alpha

    acc_ref[...] = jnp.zeros(acc_ref.shape, F32)
    p_b[...] = jnp.zeros(p_b.shape, BF16)
    load_queries(0)
    scores(0, s_a, cm_a)

    even = (s_a, cm_a, s_b, cm_b, p_a, p_b)
    odd = (s_b, cm_b, s_a, cm_a, p_b, p_a)

    def tile(i, carry):
        scores(1, s_b, cm_b)
        accf_ref[...] = acc_ref[...] + weighted_values(n_chunks - 1, p_b)
        m_ref[...] = jnp.full(m_ref.shape, -1e30, F32)
        softmax_chunk(s_a, cm_a, p_a)
        acc_ref[...] = jnp.zeros(acc_ref.shape, F32)
        step(1, *odd)

        @pl.when(i > 0)
        def _():
            finish_tile(i - 1)

        def pair(j, carry):
            c = 2 + 2 * j
            step(c, *even)
            step(c + 1, *odd)
            return carry
        lax.fori_loop(0, (n_chunks - 4) // 2, pair, 0)

        step(n_chunks - 2, *even)
        load_queries(jnp.minimum(i + 1, n_tiles - 1))
        scores(0, s_a, cm_a)
        pending = weighted_values(n_chunks - 2, p_a)
        alpha = softmax_chunk(s_b, cm_b, p_b)
        acc_ref[...] = (acc_ref[...] + pending) * alpha
        return carry
    lax.fori_loop(0, n_tiles, tile, 0)

    accf_ref[...] = acc_ref[...] + weighted_values(n_chunks - 1, p_b)
    finish_tile(n_tiles - 1)


def _diff_attention(qk, vm, gates, g_sub, lq1, lk1, lq2, lk2, lam_init, batch, seq, tq=2048, tk=512):
    head_blocks = GROUP_W // LANES
    vec = lambda a: a.reshape(1, -1).astype(F32)
    small = lambda n: pl.BlockSpec((1, n), lambda b, h: (0, 0))
    once = pl.Buffered(1)
    return pl.pallas_call(
        functools.partial(_diff_attn_kernel, lam_init=lam_init, tq=tq, tk=tk, t_chunk=512),
        grid=(batch, DA_HEADS),
        in_specs=[
            pl.BlockSpec((seq, LANES), lambda b, h: (b, h), pipeline_mode=once),
            pl.BlockSpec((seq, LANES), lambda b, h: (b, head_blocks + h), pipeline_mode=once),
            pl.BlockSpec((seq, LANES), lambda b, h: (b, h), pipeline_mode=once),
            pl.BlockSpec((seq, LANES), lambda b, h: (b, h), pipeline_mode=once),
            small(DA_V_DIM), small(DA_QK_DIM), small(DA_QK_DIM), small(DA_QK_DIM), small(DA_QK_DIM),
        ],
        out_specs=pl.BlockSpec((seq, LANES), lambda b, h: (b, h)),
        out_shape=jax.ShapeDtypeStruct((batch * seq, GROUP_W), BF16),
        scratch_shapes=[
            pltpu.VMEM((DA_V_DIM + BF16_SUBLANES, seq), BF16),
            pltpu.VMEM((2 * tq, LANES), BF16),
            pltpu.VMEM((1, 2 * tq), F32),
            pltpu.VMEM((DA_V_DIM + BF16_SUBLANES, 2 * tq), F32),
            pltpu.VMEM((DA_V_DIM + BF16_SUBLANES, 2 * tq), F32),
            pltpu.VMEM((tk, 2 * tq), F32), pltpu.VMEM((tk, 2 * tq), F32),
            pltpu.VMEM((1, 2 * tq), F32), pltpu.VMEM((1, 2 * tq), F32),
            pltpu.VMEM((tk, 2 * tq), BF16), pltpu.VMEM((tk, 2 * tq), BF16),
        ],
        compiler_params=_cparams(("parallel", "parallel")),
        name="diff_attention",
    )(qk, qk, vm, gates, vec(g_sub), vec(lq1), vec(lk1), vec(lq2), vec(lk2))


def _filter_kernel(z_ref, zs_ref, w1_ref, b1_ref, w2_ref, b2_ref, w3_ref, b3_ref, w4f_ref, w4b_ref,
                   fr_ref, df_ref, db_ref, o_ref):
    hp = lax.Precision.HIGHEST
    fr = fr_ref[...]

    def mlp(z):
        h = jnp.sin(fr * (jnp.dot(w1_ref[...], z, precision=hp, preferred_element_type=F32) + b1_ref[...]))
        h = jnp.sin(fr * (jnp.dot(w2_ref[...], h, precision=hp, preferred_element_type=F32) + b2_ref[...]))
        return jnp.sin(fr * (jnp.dot(w3_ref[...], h, precision=hp, preferred_element_type=F32) + b3_ref[...]))

    def dot_split(a, b):
        a_hi = a.astype(BF16)
        a_lo = (a - a_hi.astype(F32)).astype(BF16)
        b_hi = b.astype(BF16)
        b_lo = (b - b_hi.astype(F32)).astype(BF16)
        return (jnp.dot(a_hi, b_hi, preferred_element_type=F32)
                + (jnp.dot(a_hi, b_lo, preferred_element_type=F32)
                   + jnp.dot(a_lo, b_hi, preferred_element_type=F32)))

    z = z_ref[...]
    zs = zs_ref[...]
    tl = z.shape[1]
    ff = dot_split(w4f_ref[...], mlp(z))
    fb = dot_split(w4b_ref[...], mlp(zs))
    col = pl.program_id(0) * tl + lax.broadcasted_iota(jnp.int32, (1, tl), 1)
    dec_f = jnp.exp(-df_ref[...] * z[0:1, :])
    dec_b = jnp.where(col > 0, jnp.exp(-db_ref[...] * zs[0:1, :]), 0.0)
    c = HY_CH
    o_ref[0 * c:1 * c, :] = ff[:c] * dec_f
    o_ref[1 * c:2 * c, :] = fb[:c] * dec_b
    o_ref[2 * c:3 * c, :] = ff[c:] * dec_f
    o_ref[3 * c:4 * c, :] = fb[c:] * dec_b


def _hyena_filters(zt, zst, deltas_f, deltas_b, w1, b1, w2, b2, w3, b3, w4, freq, tl=512):
    seq = zt.shape[1]
    k = HF_ORDER
    w1t = jnp.zeros((k, k), F32).at[:, :HF_EMB].set(w1.astype(F32).T)
    w4r = w4.astype(F32).reshape(k, 2, 2, HY_CH)
    w4ft = w4r[:, :, 0, :].reshape(k, 2 * HY_CH).T
    w4bt = w4r[:, :, 1, :].reshape(k, 2 * HY_CH).T
    colv = lambda a: a.reshape(-1, 1).astype(F32)
    full = lambda shape: pl.BlockSpec(shape, lambda i: (0, 0))
    return pl.pallas_call(
        _filter_kernel,
        grid=(seq // tl,),
        in_specs=[
            pl.BlockSpec((k, tl), lambda i: (0, i)), pl.BlockSpec((k, tl), lambda i: (0, i)),
            full((k, k)), full((k, 1)), full((k, k)), full((k, 1)), full((k, k)), full((k, 1)),
            full((2 * HY_CH, k)), full((2 * HY_CH, k)), full((k, 1)), full((HY_CH, 1)), full((HY_CH, 1)),
        ],
        out_specs=pl.BlockSpec((4 * HY_CH, tl), lambda i: (0, i)),
        out_shape=jax.ShapeDtypeStruct((4 * HY_CH, seq), F32),
        compiler_params=_cparams(("parallel",)),
        name="hyena_filters",
    )(zt, zst, w1t, colv(b1), w2.astype(F32).T, colv(b2), w3.astype(F32).T, colv(b3), w4ft, w4bt,
      colv(freq), deltas_f, deltas_b)


def _dft_tables():
    k1 = np.arange(FFT_R, dtype=np.int64)[:, None]
    n1 = np.arange(FFT_HALF, dtype=np.int64)[None, :]
    a1 = -2.0 * np.pi * ((k1 * n1) % FFT_R) / FFT_R
    f1 = np.concatenate([np.cos(a1), np.sin(a1)], axis=0)
    f1_inv = np.concatenate([np.cos(a1).T, np.sin(a1).T], axis=1)
    n2 = np.arange(FFT_R, dtype=np.int64)[None, :]
    at = -2.0 * np.pi * ((k1 * n2) % FFT_N) / FFT_N
    k2 = np.arange(FFT_R, dtype=np.int64)[None, :]
    a2 = -2.0 * np.pi * ((n2.T * k2) % FFT_R) / FFT_R
    fr, fi = np.cos(a2), np.sin(a2)
    c2 = np.block([[fr, fi], [-fi, fr]])
    c2_inv = np.block([[fr, -fi], [fi, fr]])
    as_bf16 = lambda a: jnp.asarray(a, F32).astype(BF16)
    return dict(f1=as_bf16(f1), f1_inv=as_bf16(f1_inv), c2=as_bf16(c2), c2_inv=as_bf16(c2_inv),
                tr=jnp.asarray(np.cos(at), F32), ti=jnp.asarray(np.sin(at), F32))


def _dft_many(xs, f1_ref, c2_ref, tr, ti):
    r = FFT_R
    xx = jnp.concatenate([x.astype(BF16) for x in xs], axis=1)
    a = jnp.dot(f1_ref[...], xx, preferred_element_type=F32)
    rows = []
    for s in range(len(xs)):
        ar, ai = a[:r, s * r:(s + 1) * r], a[r:, s * r:(s + 1) * r]
        rows.append(jnp.concatenate([ar * tr - ai * ti, ar * ti + ai * tr], axis=1).astype(BF16))
    return jnp.dot(jnp.concatenate(rows, axis=0), c2_ref[...], preferred_element_type=F32)


def _idft_many(ys, f1i_ref, c2i_ref, tr, ti):
    r = FFT_R
    y = jnp.concatenate([v.astype(BF16) for v in ys], axis=0)
    b = jnp.dot(y, c2i_ref[...], preferred_element_type=F32)
    outs_r, outs_i = [], []
    for s in range(len(ys)):
        br, bi = b[s * r:(s + 1) * r, :r], b[s * r:(s + 1) * r, r:]
        outs_r.append((br * tr + bi * ti).astype(BF16))
        outs_i.append((bi * tr - br * ti).astype(BF16))
    rhs = jnp.concatenate([jnp.concatenate(outs_r, axis=1), jnp.concatenate(outs_i, axis=1)], axis=0)
    return jnp.dot(f1i_ref[...], rhs, preferred_element_type=F32)


def _filter_spectrum_kernel(hf_ref, hb_ref, f1_ref, c2_ref, tr_ref, ti_ref, o_ref):
    r = FFT_R
    cb = hf_ref.shape[2]
    xs = [ref[0, 0, ci] for ci in range(cb) for ref in (hf_ref, hb_ref)]
    x = _dft_many(xs, f1_ref, c2_ref, tr_ref[...], ti_ref[...])
    for ci in range(cb):
        xf, xb = x[2 * ci * r:(2 * ci + 1) * r], x[(2 * ci + 1) * r:(2 * ci + 2) * r]
        o_ref[0, ci] = jnp.concatenate([xf[:, :r] + xb[:, :r], xf[:, r:] - xb[:, r:]],
                                       axis=1).astype(o_ref.dtype)


def _filter_spectrum(filt5, tabs):
    n_ord, _, n_ch, rows, _ = filt5.shape
    cb = HY_CBLK
    const = lambda shape: pl.BlockSpec(shape, lambda o, c: (0, 0))
    return pl.pallas_call(
        _filter_spectrum_kernel,
        grid=(n_ord, n_ch // cb),
        in_specs=[
            pl.BlockSpec((1, 1, cb, rows, LANES), lambda o, c: (o, 0, c, 0, 0)),
            pl.BlockSpec((1, 1, cb, rows, LANES), lambda o, c: (o, 1, c, 0, 0)),
            const((2 * FFT_R, FFT_HALF)), const((2 * FFT_R, 2 * FFT_R)),
            const((FFT_R, FFT_R)), const((FFT_R, FFT_R)),
        ],
        out_specs=pl.BlockSpec((1, cb, FFT_R, 2 * FFT_R), lambda o, c: (o, c, 0, 0)),
        out_shape=jax.ShapeDtypeStruct((n_ord, n_ch, FFT_R, 2 * FFT_R), BF16),
        compiler_params=_cparams(("parallel", "parallel")),
        name="filter_spectrum",
    )(filt5, filt5, tabs["f1"], tabs["c2"], tabs["tr"], tabs["ti"])


def _short_conv_mat(u, w0, w1, w2, bias):
    rows, lanes = u.shape
    lane = lax.broadcasted_iota(jnp.int32, u.shape, 1)
    row = lax.broadcasted_iota(jnp.int32, u.shape, 0)
    prev = pltpu.roll(u, 1, 1)
    prev = jnp.where(lane == 0, jnp.where(row == 0, 0.0, pltpu.roll(prev, 1, 0)), prev)
    nxt = pltpu.roll(u, lanes - 1, 1)
    nxt = jnp.where(lane == lanes - 1, jnp.where(row == rows - 1, 0.0, pltpu.roll(nxt, rows - 1, 0)), nxt)
    return w0 * prev + w1 * u + w2 * nxt + bias


def _hyena_conv_kernel(cw_ref, d_ref, u_ref, gate_ref, *refs, conv_in, gated_out, u_ch0, gate_ch0, order_ch0):
    if gated_out:
        g2_ref, k_ref, f1_ref, f1i_ref, c2_ref, c2i_ref, tr_ref, ti_ref, o_ref = refs
    else:
        k_ref, f1_ref, f1i_ref, c2_ref, c2i_ref, tr_ref, ti_ref, o_ref = refs
        g2_ref = None
    r = FFT_R
    tr, ti = tr_ref[...], ti_ref[...]
    batch, cb = u_ref.shape[0], u_ref.shape[1]
    c0 = pl.program_id(0) * cb
    signals = [(ci, b) for ci in range(cb) for b in range(batch)]

    def conv_taps(ref, b, ci, ch):
        return _short_conv_mat(ref[b, ci], cw_ref[0, ch], cw_ref[1, ch], cw_ref[2, ch], cw_ref[3, ch])

    def long_conv_input(ci, b):
        return conv_taps(u_ref, b, ci, u_ch0 + c0 + ci) if conv_in else u_ref[b, ci]

    us = [long_conv_input(ci, b) for ci, b in signals]
    x = _dft_many(us, f1_ref, c2_ref, tr, ti)
    ys = []
    for ci in range(cb):
        kk = k_ref[0, ci].astype(F32)
        kr, ki = kk[:, :r], kk[:, r:]
        for b in range(batch):
            s = ci * batch + b
            xr, xi = x[s * r:(s + 1) * r, :r], x[s * r:(s + 1) * r, r:]
            ys.append(jnp.concatenate([xr * kr - xi * ki, xr * ki + xi * kr], axis=1))
    y = _idft_many(ys, f1i_ref, c2i_ref, tr, ti)
    for s, (ci, b) in enumerate(signals):
        gate = conv_taps(gate_ref, b, ci, gate_ch0 + c0 + ci)
        skip = d_ref[0, order_ch0 + c0 + ci] * us[s]
        res = gate * (y[:, s * r:(s + 1) * r] * (1.0 / FFT_N) + skip)
        if gated_out:
            res = res * _silu(g2_ref[b, ci])
        o_ref[b, ci] = res.astype(o_ref.dtype)


def _hyena_conv(conv_wb, d_skip, u_arr, u_ch0, gate_arr, gate_ch0, g2, spec, order, tabs, out_dtype, conv_in):
    batch = u_arr.shape[0]
    assert batch == 2 and u_arr.shape[2] == FFT_HALF
    cb = HY_CBLK
    smem = pl.BlockSpec(memory_space=pltpu.SMEM)
    chan = lambda ch0: pl.BlockSpec((batch, cb, FFT_HALF, LANES), lambda c: (0, ch0 // cb + c, 0, 0))
    const = lambda shape: pl.BlockSpec(shape, lambda c: (0, 0))
    in_specs = [smem, smem, chan(u_ch0), chan(gate_ch0)]
    args = [conv_wb, d_skip, u_arr, gate_arr]
    if g2 is not None:
        in_specs.append(chan(g2[1]))
        args.append(g2[0])
    in_specs += [
        pl.BlockSpec((1, cb, FFT_R, 2 * FFT_R), lambda c: (order, c, 0, 0)),
        const((2 * FFT_R, FFT_HALF)), const((FFT_HALF, 2 * FFT_R)),
        const((2 * FFT_R, 2 * FFT_R)), const((2 * FFT_R, 2 * FFT_R)),
        const((FFT_R, FFT_R)), const((FFT_R, FFT_R)),
    ]
    args += [spec, tabs["f1"], tabs["f1_inv"], tabs["c2"], tabs["c2_inv"], tabs["tr"], tabs["ti"]]
    return pl.pallas_call(
        functools.partial(_hyena_conv_kernel, conv_in=conv_in, gated_out=g2 is not None,
                          u_ch0=u_ch0, gate_ch0=gate_ch0, order_ch0=order * HY_CH),
        grid=(HY_CH // cb,),
        in_specs=in_specs,
        out_specs=pl.BlockSpec((batch, cb, FFT_HALF, LANES), lambda c: (0, c, 0, 0)),
        out_shape=jax.ShapeDtypeStruct((batch, HY_CH, FFT_HALF, LANES), out_dtype),
        compiler_params=_cparams(("parallel",)),
        name="hyena_conv_gated" if g2 is not None else "hyena_conv",
    )(*args)


def _out_kernel(ya_ref, ybt_ref, mq_ref, mg_ref, mk_ref, mv_ref, w_ref, x_ref, gf_ref, o_ref, *, final):
    gw = GROUP_W
    acc = x_ref[...]
    acc += jnp.dot(ya_ref[...], w_ref[0:gw, :], preferred_element_type=F32)
    acc += lax.dot_general(ybt_ref[0], w_ref[gw:2 * gw, :], (((0,), (0,)), ((), ())),
                           preferred_element_type=F32)
    hd = MEM_HEAD_DIM
    heads = []
    for h in range(MEM_HEADS):
        sl = slice(h * hd, (h + 1) * hd)
        s = lax.dot_general(mq_ref[:, sl], mk_ref[:, sl], (((1,), (1,)), ((), ())),
                            preferred_element_type=F32) * (hd ** -0.5)
        e = jnp.exp(s - jnp.max(s, axis=-1, keepdims=True))
        p = e / jnp.sum(e, axis=-1, keepdims=True)
        oh = jnp.dot(p.astype(BF16), mv_ref[:, sl], preferred_element_type=F32)
        heads.append((oh * _silu(mg_ref[:, sl])).astype(BF16))
    acc += jnp.dot(jnp.concatenate(heads, axis=1), w_ref[2 * gw:3 * gw, :], preferred_element_type=F32)
    if final:
        acc = _rms_norm_rows(acc, gf_ref[...])
    o_ref[...] = acc


def _out_proj(ya, ybt, vm, feats, mg_blk, mkv, w_out, x, g_final, final, seq, mem_len, tm=512):
    m, d = x.shape
    gw = GROUP_W
    per_batch = seq // tm
    return pl.pallas_call(
        functools.partial(_out_kernel, final=final),
        grid=(m // tm,),
        in_specs=[
            pl.BlockSpec((tm, gw), lambda i: (i, 0)),
            pl.BlockSpec((1, gw, tm), lambda i: (i // per_batch, 0, i % per_batch)),
            pl.BlockSpec((tm, gw), lambda i: (i, 1)),
            pl.BlockSpec((tm, gw), lambda i: (i, mg_blk)),
            pl.BlockSpec((mem_len, gw), lambda i: (i // per_batch, 0)),
            pl.BlockSpec((mem_len, gw), lambda i: (i // per_batch, 1)),
            pl.BlockSpec((3 * gw, d), lambda i: (0, 0)),
            pl.BlockSpec((tm, d), lambda i: (i, 0)),
            pl.BlockSpec((1, d), lambda i: (0, 0)),
        ],
        out_specs=pl.BlockSpec((tm, d), lambda i: (i, 0)),
        out_shape=jax.ShapeDtypeStruct((m, d), F32),
        compiler_params=_cparams(("parallel",)),
        name="out_proj_final" if final else "out_proj",
    )(ya, ybt, vm, feats, mkv, mkv, w_out, x, g_final.reshape(1, d).astype(F32))


def _rope_tables(seq):
    pos = np.arange(seq, dtype=np.float64)
    inv_freq = ROPE_THETA ** (-np.arange(0, DA_QK_DIM, 2, dtype=np.float64) / DA_QK_DIM)
    ang = pos[:, None] * inv_freq[None, :]
    cos, sin = np.cos(ang), np.sin(ang)
    reps = LANES // DA_QK_DIM
    return (jnp.asarray(np.tile(np.concatenate([cos, cos], axis=-1), (1, reps)), F32),
            jnp.asarray(np.tile(np.concatenate([-sin, sin], axis=-1), (1, reps)), F32))


def _filter_features(seq):
    t = np.linspace(0.0, 1.0, seq)[:, None]
    w = 2.0 * math.pi * np.arange(seq, dtype=np.float64)[:, None] / seq
    f = np.linspace(1e-4, HF_BANDS - 1, HF_BANDS)[None, :]
    z = np.concatenate([t, np.cos(f * w), -np.sin(f * w)], axis=-1)
    z = np.pad(z, ((0, 0), (0, HF_ORDER - HF_EMB)))
    zs = np.concatenate([z[:1], z[:-1]], axis=0)
    max_decay = math.log(HF_TARGET) / HF_FAST
    min_decay = math.log(HF_TARGET) / HF_SLOW
    deltas = np.abs(np.linspace(min_decay, max_decay, HY_CH))
    as_f32 = lambda a: jnp.asarray(np.ascontiguousarray(a), F32)
    return as_f32(z.T), as_f32(zs.T), as_f32(deltas[:, None]), as_f32(deltas[::-1][:, None])


def kernel(x, mem, g_norm, w_in, da_lam_q1, da_lam_k1, da_lam_q2, da_lam_k2, da_subln_g, hy_conv_w, hy_conv_b, hf_w1, hf_b1, hf_w2, hf_b2, hf_w3, hf_b3, hf_w4, hf_freq, hy_skip, g_mem, w_mem_kv, w_out, g_final):
    batch, seq, d = x.shape
    assert seq == SEQ_LEN
    mem_len = mem.shape[1]
    depth = w_in.shape[0]
    gw = GROUP_W
    xf = x.reshape(batch * seq, d).astype(F32)
    memf = mem.reshape(batch * mem_len, d).astype(F32)
    rope_tabs = _rope_tables(seq)
    zt, zst, deltas_f, deltas_b = _filter_features(seq)
    tabs = _dft_tables()

    for l in range(depth):
        w = w_in[l].astype(BF16)
        w_tok = jnp.concatenate([w[:, :3 * gw], w[:, 8 * gw:9 * gw], w[:, 3 * gw:4 * gw], w[:, 9 * gw:]], axis=1)
        w_hy_t = w[:, 4 * gw:8 * gw].T
        qk, vm, gates, hy = _input_proj(xf, g_norm[l], w_tok, w_hy_t, rope_tabs,
                                        DA_QK_DIM ** -0.5 * math.log2(math.e),
                                        batch, seq)
        hy = hy.reshape(batch, 4 * HY_CH, FFT_HALF, LANES)
        mkv = _norm_matmul(memf, g_mem[l], w_mem_kv[l].astype(BF16), BF16, "proj_mem_kv")

        lam_init = 0.8 - 0.6 * math.exp(-0.3 * l)
        ya = _diff_attention(qk, vm, gates, da_subln_g[l], da_lam_q1[l], da_lam_k1[l],
                             da_lam_q2[l], da_lam_k2[l], lam_init, batch, seq)

        filt = _hyena_filters(zt, zst, deltas_f, deltas_b, hf_w1[l], hf_b1[l], hf_w2[l], hf_b2[l],
                              hf_w3[l], hf_b3[l], hf_w4[l], hf_freq[l])
        spec = _filter_spectrum(filt.reshape(2, 2, HY_CH, FFT_HALF, LANES), tabs)
        conv_wb = jnp.concatenate([hy_conv_w[l], hy_conv_b[l][None, :]], axis=0).astype(F32)
        d_skip = hy_skip[l].reshape(1, -1).astype(F32)
        z1 = _hyena_conv(conv_wb, d_skip, hy, 0, hy, HY_CH, None, spec, 0, tabs, F32, conv_in=True)
        ybt = _hyena_conv(conv_wb, d_skip, z1, 0, hy, 2 * HY_CH, (hy, 3 * HY_CH), spec, 1, tabs, BF16,
                          conv_in=False)
        ybt = ybt.reshape(batch, HY_CH, seq)

        xf = _out_proj(ya, ybt, vm, gates, 1, mkv, w_out[l].astype(BF16), xf, g_final,
                       l == depth - 1, seq, mem_len)
    return xf.reshape(batch, seq, d)
```

```python
import functools
import math

import jax
import jax.numpy as jnp
import numpy as np
from jax import lax
from jax.experimental import pallas as pl
from jax.experimental.pallas import tpu as pltpu

F32 = jnp.float32
BF16 = jnp.bfloat16

GROUP_W = 512
DA_HEADS = 4
DA_QK_DIM = 64
DA_V_DIM = 128
HY_CH = 512
HF_EMB = 33
HF_BANDS = 16
HF_ORDER = 64
HF_TARGET = 1e-2
HF_FAST = 0.3
HF_SLOW = 1.5
MEM_HEADS = 4
MEM_HEAD_DIM = 128
ROPE_THETA = 10000.0
EPS = 1e-6

LANES = 128
BF16_SUBLANES = 16
VMEM_LIMIT = 56 * 1024 * 1024

FFT_R = 128
FFT_N = FFT_R * FFT_R
SEQ_LEN = FFT_N // 2
FFT_HALF = FFT_R // 2
HY_CBLK = 16


def _cparams(sem):
    return pltpu.CompilerParams(dimension_semantics=sem, vmem_limit_bytes=VMEM_LIMIT)


def _silu(g):
    return g * (1.0 / (1.0 + jnp.exp(-g)))


def _rms_norm_rows(x, g):
    ms = jnp.mean(x * x, axis=-1, keepdims=True)
    return x * lax.rsqrt(ms + EPS) * g


def _norm_matmul_kernel(x_ref, g_ref, w_ref, *refs, rope, q_scale):
    if rope:
        cos_ref, sin_ref, o_ref, h_ref = refs
    else:
        o_ref, h_ref = refs
    j = pl.program_id(1)

    @pl.when(j == 0)
    def _():
        h_ref[...] = _rms_norm_rows(x_ref[...], g_ref[...]).astype(BF16)

    acc = jnp.dot(h_ref[...], w_ref[...], preferred_element_type=F32)
    if rope:
        tm, tn = acc.shape
        scale = jnp.where(j == 0, q_scale, 1.0).astype(F32)
        cos = cos_ref[...]
        sin = sin_ref[...]
        lane = lax.broadcasted_iota(jnp.int32, (tm, LANES), 1)
        first_half = (lane % DA_QK_DIM) < (DA_QK_DIM // 2)
        for c in range(tn // LANES):
            a = acc[:, c * LANES:(c + 1) * LANES]
            partner = jnp.where(first_half,
                                pltpu.roll(a, LANES - DA_QK_DIM // 2, 1),
                                pltpu.roll(a, DA_QK_DIM // 2, 1))
            o_ref[:, c * LANES:(c + 1) * LANES] = ((a * cos + partner * sin) * scale).astype(o_ref.dtype)
    else:
        o_ref[...] = acc.astype(o_ref.dtype)


def _norm_matmul(x, g, w, out_dtype, name, rope_tabs=None, q_scale=1.0, tm=1024, tn=512):
    m, d = x.shape
    n = w.shape[1]
    tm = min(tm, m)
    assert m % tm == 0 and n % tn == 0
    in_specs = [
        pl.BlockSpec((tm, d), lambda i, j: (i, 0)),
        pl.BlockSpec((1, d), lambda i, j: (0, 0)),
        pl.BlockSpec((d, tn), lambda i, j: (0, j)),
    ]
    args = [x, g.reshape(1, d), w]
    if rope_tabs is not None:
        n_pos = rope_tabs[0].shape[0] // tm
        for t in rope_tabs:
            in_specs.append(pl.BlockSpec((tm, LANES), lambda i, j: (i % n_pos, 0)))
            args.append(t)
    return pl.pallas_call(
        functools.partial(_norm_matmul_kernel, rope=rope_tabs is not None, q_scale=q_scale),
        grid=(m // tm, n // tn),
        in_specs=in_specs,
        out_specs=pl.BlockSpec((tm, tn), lambda i, j: (i, j)),
        out_shape=jax.ShapeDtypeStruct((m, n), out_dtype),
        scratch_shapes=[pltpu.VMEM((tm, d), BF16)],
        compiler_params=_cparams(("parallel", "arbitrary")),
        name=name,
    )(*args)


PROJ_TN = 1024
PROJ_TOKEN_TILES = 3
PROJ_CHANNEL_TILES = 2


def _input_proj_kernel(x_ref, g_ref, w_ref, wt_ref, cos_ref, sin_ref, qk_ref, vm_ref, gate_ref, hy_ref, h_ref,
                       *, q_scale):
    j = pl.program_id(1)
    tn = PROJ_TN

    @pl.when(j == 0)
    def _():
        h_ref[...] = _rms_norm_rows(x_ref[...], g_ref[...]).astype(BF16)

    def token_tile(t):
        return jnp.dot(h_ref[...], w_ref[:, t * tn:(t + 1) * tn], preferred_element_type=F32)

    @pl.when(j == 0)
    def _():
        acc = token_tile(0)
        tm = acc.shape[0]
        cos = cos_ref[...]
        sin = sin_ref[...]
        lane = lax.broadcasted_iota(jnp.int32, (tm, LANES), 1)
        first_half = (lane % DA_QK_DIM) < (DA_QK_DIM // 2)
        for c in range(tn // LANES):
            scale = q_scale if c * LANES < GROUP_W else 1.0
            a = acc[:, c * LANES:(c + 1) * LANES]
            partner = jnp.where(first_half,
                                pltpu.roll(a, LANES - DA_QK_DIM // 2, 1),
                                pltpu.roll(a, DA_QK_DIM // 2, 1))
            qk_ref[:, c * LANES:(c + 1) * LANES] = ((a * cos + partner * sin) * scale).astype(qk_ref.dtype)

    @pl.when(j == 1)
    def _():
        vm_ref[...] = token_tile(1).astype(vm_ref.dtype)

    @pl.when(j == 2)
    def _():
        gate_ref[...] = token_tile(2)

    for t in range(PROJ_CHANNEL_TILES):
        @pl.when(j == PROJ_TOKEN_TILES + t)
        def _(t=t):
            hy_ref[0] = lax.dot_general(wt_ref[t * tn:(t + 1) * tn, :], h_ref[...], (((1,), (1,)), ((), ())),
                                        preferred_element_type=F32)


def _input_proj(x, g, w_tok, w_ch_t, rope_tabs, q_scale, batch, seq, tm=1024):
    m, d = x.shape
    tn = PROJ_TN
    assert w_tok.shape == (d, PROJ_TOKEN_TILES * tn) and w_ch_t.shape == (PROJ_CHANNEL_TILES * tn, d)
    per_batch = seq // tm
    once = pl.Buffered(1)
    clip = lambda j, lo, n: jnp.clip(j - lo, 0, n - 1)
    return pl.pallas_call(
        functools.partial(_input_proj_kernel, q_scale=q_scale),
        grid=(m // tm, PROJ_TOKEN_TILES + PROJ_CHANNEL_TILES),
        in_specs=[
            pl.BlockSpec((tm, d), lambda i, j: (i, 0)),
            pl.BlockSpec((1, d), lambda i, j: (0, 0)),
            pl.BlockSpec(w_tok.shape, lambda i, j: (0, 0), pipeline_mode=once),
            pl.BlockSpec(w_ch_t.shape, lambda i, j: (0, 0), pipeline_mode=once),
            pl.BlockSpec((tm, LANES), lambda i, j: (i % per_batch, 0)),
            pl.BlockSpec((tm, LANES), lambda i, j: (i % per_batch, 0)),
        ],
        out_specs=[
            pl.BlockSpec((tm, tn), lambda i, j: (i, 0)),
            pl.BlockSpec((tm, tn), lambda i, j: (i, 0)),
            pl.BlockSpec((tm, tn), lambda i, j: (i, 0)),
            pl.BlockSpec((1, tn, tm), lambda i, j: (i // per_batch, clip(j, PROJ_TOKEN_TILES, PROJ_CHANNEL_TILES),
                                                    i % per_batch)),
        ],
        out_shape=[
            jax.ShapeDtypeStruct((m, tn), BF16),
            jax.ShapeDtypeStruct((m, tn), BF16),
            jax.ShapeDtypeStruct((m, tn), F32),
            jax.ShapeDtypeStruct((batch, PROJ_CHANNEL_TILES * tn, seq), F32),
        ],
        scratch_shapes=[pltpu.VMEM((tm, d), BF16)],
        compiler_params=_cparams(("parallel", "arbitrary")),
        name="input_proj",
    )(x, g.reshape(1, d), w_tok, w_ch_t, rope_tabs[0], rope_tabs[1])


def _diff_attn_kernel(q_ref, k_ref, v_ref, gate_ref, gsub_ref, lq1_ref, lk1_ref, lq2_ref, lk2_ref,
                      o_ref, vt_ref, q2_ref, m_ref, acc_ref, accf_ref, s_a, s_b, cm_a, cm_b, p_a, p_b,
                      *, lam_init, tq, tk, t_chunk):
    seq = k_ref.shape[0]
    dv = DA_V_DIM
    n_tiles = seq // tq
    n_chunks = seq // tk
    assert n_chunks % 2 == 0 and n_chunks >= 4

    row = lax.broadcasted_iota(jnp.int32, (BF16_SUBLANES, t_chunk), 0)
    ones_row = jnp.where(row == 0, 1.0, 0.0).astype(BF16)
    for c in range(seq // t_chunk):
        blk = v_ref[c * t_chunk:(c + 1) * t_chunk, :].astype(F32)
        vt_ref[0:dv, c * t_chunk:(c + 1) * t_chunk] = blk.T.astype(BF16)
        vt_ref[dv:dv + BF16_SUBLANES, c * t_chunk:(c + 1) * t_chunk] = ones_row

    lam = (jnp.exp(jnp.sum(lq1_ref[...] * lk1_ref[...], axis=-1, keepdims=True))
           - jnp.exp(jnp.sum(lq2_ref[...] * lk2_ref[...], axis=-1, keepdims=True)) + lam_init)

    def tile_rows(i):
        return pl.ds(pl.multiple_of(i * tq, tq), tq)

    def load_queries(i):
        q = q_ref[tile_rows(i), :]
        lane = lax.broadcasted_iota(jnp.int32, q.shape, 1)
        zero = jnp.zeros_like(q)
        q2_ref[...] = jnp.concatenate([jnp.where(lane < DA_QK_DIM, q, zero),
                                       jnp.where(lane >= DA_QK_DIM, q, zero)], axis=0)

    def finish_tile(i):
        for g in range(tq // LANES):
            rows = pl.ds(pl.multiple_of(i * tq + g * LANES, LANES), LANES)
            c0 = slice(g * LANES, (g + 1) * LANES)
            c1 = slice(tq + g * LANES, tq + (g + 1) * LANES)
            ot = (accf_ref[0:dv, c0] / accf_ref[dv:dv + 1, c0]
                  - lam * (accf_ref[0:dv, c1] / accf_ref[dv:dv + 1, c1]))
            y = _rms_norm_rows(ot.T, gsub_ref[...]) * (1.0 - lam_init)
            o_ref[rows, :] = (y * _silu(gate_ref[rows, :])).astype(o_ref.dtype)

    def scores(c, s_ref, cm_ref):
        kc = k_ref[pl.ds(pl.multiple_of(c * tk, tk), tk), :]
        s = lax.dot_general(kc, q2_ref[...], (((1,), (1,)), ((), ())), preferred_element_type=F32)
        s_ref[...] = s
        cm_ref[...] = jnp.max(s, axis=0, keepdims=True)

    def weighted_values(c, p_ref):
        vtc = vt_ref[:, pl.ds(pl.multiple_of(c * tk, tk), tk)]
        return jnp.dot(vtc, p_ref[...], preferred_element_type=F32)

    def softmax_chunk(s_ref, cm_ref, p_ref):
        m_prev = m_ref[...]
        m_new = jnp.maximum(m_prev, cm_ref[...])
        p_ref[...] = jnp.exp2(s_ref[...] - m_new).astype(BF16)
        m_ref[...] = m_new
        return jnp.exp2(m_prev - m_new)

    def step(c, s_cur, cm_cur, s_next, cm_next, p_cur, p_prev):
        scores(c + 1, s_next, cm_next)
        pending = weighted_values(c - 1, p_prev)
        alpha = softmax_chunk(s_cur, cm_cur, p_cur)
        acc_ref[...] = (acc_ref[...] + pending) * alpha

    acc_ref[...] = jnp.zeros(acc_ref.shape, F32)
    p_b[...] = jnp.zeros(p_b.shape, BF16)
    load_queries(0)
    scores(0, s_a, cm_a)

    even = (s_a, cm_a, s_b, cm_b, p_a, p_b)
    odd = (s_b, cm_b, s_a, cm_a, p_b, p_a)

    def tile(i, carry):
        scores(1, s_b, cm_b)
        accf_ref[...] = acc_ref[...] + weighted_values(n_chunks - 1, p_b)
        m_ref[...] = jnp.full(m_ref.shape, -1e30, F32)
        softmax_chunk(s_a, cm_a, p_a)
        acc_ref[...] = jnp.zeros(acc_ref.shape, F32)
        step(1, *odd)

        @pl.when(i > 0)
        def _():
            finish_tile(i - 1)

        def pair(j, carry):
            c = 2 + 2 * j
            step(c, *even)
            step(c + 1, *odd)
            return carry
        lax.fori_loop(0, (n_chunks - 4) // 2, pair, 0)

        step(n_chunks - 2, *even)
        load_queries(jnp.minimum(i + 1, n_tiles - 1))
        scores(0, s_a, cm_a)
        pending = weighted_values(n_chunks - 2, p_a)
        alpha = softmax_chunk(s_b, cm_b, p_b)
        acc_ref[...] = (acc_ref[...] + pending) * alpha
        return carry
    lax.fori_loop(0, n_tiles, tile, 0)

    accf_ref[...] = acc_ref[...] + weighted_values(n_chunks - 1, p_b)
    finish_tile(n_tiles - 1)


def _diff_attention(qk, vm, gates, g_sub, lq1, lk1, lq2, lk2, lam_init, batch, seq, tq=2048, tk=512):
    head_blocks = GROUP_W // LANES
    vec = lambda a: a.reshape(1, -1).astype(F32)
    small = lambda n: pl.BlockSpec((1, n), lambda b, h: (0, 0))
    once = pl.Buffered(1)
    return pl.pallas_call(
        functools.partial(_diff_attn_kernel, lam_init=lam_init, tq=tq, tk=tk, t_chunk=512),
        grid=(batch, DA_HEADS),
        in_specs=[
            pl.BlockSpec((seq, LANES), lambda b, h: (b, h), pipeline_mode=once),
            pl.BlockSpec((seq, LANES), lambda b, h: (b, head_blocks + h), pipeline_mode=once),
            pl.BlockSpec((seq, LANES), lambda b, h: (b, h), pipeline_mode=once),
            pl.BlockSpec((seq, LANES), lambda b, h: (b, h), pipeline_mode=once),
            small(DA_V_DIM), small(DA_QK_DIM), small(DA_QK_DIM), small(DA_QK_DIM), small(DA_QK_DIM),
        ],
        out_specs=pl.BlockSpec((seq, LANES), lambda b, h: (b, h)),
        out_shape=jax.ShapeDtypeStruct((batch * seq, GROUP_W), BF16),
        scratch_shapes=[
            pltpu.VMEM((DA_V_DIM + BF16_SUBLANES, seq), BF16),
            pltpu.VMEM((2 * tq, LANES), BF16),
            pltpu.VMEM((1, 2 * tq), F32),
            pltpu.VMEM((DA_V_DIM + BF16_SUBLANES, 2 * tq), F32),
            pltpu.VMEM((DA_V_DIM + BF16_SUBLANES, 2 * tq), F32),
            pltpu.VMEM((tk, 2 * tq), F32), pltpu.VMEM((tk, 2 * tq), F32),
            pltpu.VMEM((1, 2 * tq), F32), pltpu.VMEM((1, 2 * tq), F32),
            pltpu.VMEM((tk, 2 * tq), BF16), pltpu.VMEM((tk, 2 * tq), BF16),
        ],
        compiler_params=_cparams(("parallel", "parallel")),
        name="diff_attention",
    )(qk, qk, vm, gates, vec(g_sub), vec(lq1), vec(lk1), vec(lq2), vec(lk2))


def _filter_kernel(z_ref, zs_ref, w1_ref, b1_ref, w2_ref, b2_ref, w3_ref, b3_ref, w4f_ref, w4b_ref,
                   fr_ref, df_ref, db_ref, o_ref):
    hp = lax.Precision.HIGHEST
    fr = fr_ref[...]

    def mlp(z):
        h = jnp.sin(fr * (jnp.dot(w1_ref[...], z, precision=hp, preferred_element_type=F32) + b1_ref[...]))
        h = jnp.sin(fr * (jnp.dot(w2_ref[...], h, precision=hp, preferred_element_type=F32) + b2_ref[...]))
        return jnp.sin(fr * (jnp.dot(w3_ref[...], h, precision=hp, preferred_element_type=F32) + b3_ref[...]))

    def dot_split(a, b):
        a_hi = a.astype(BF16)
        a_lo = (a - a_hi.astype(F32)).astype(BF16)
        b_hi = b.astype(BF16)
        b_lo = (b - b_hi.astype(F32)).astype(BF16)
        return (jnp.dot(a_hi, b_hi, preferred_element_type=F32)
                + (jnp.dot(a_hi, b_lo, preferred_element_type=F32)
                   + jnp.dot(a_lo, b_hi, preferred_element_type=F32)))

    z = z_ref[...]
    zs = zs_ref[...]
    tl = z.shape[1]
    ff = dot_split(w4f_ref[...], mlp(z))
    fb = dot_split(w4b_ref[...], mlp(zs))
    col = pl.program_id(0) * tl + lax.broadcasted_iota(jnp.int32, (1, tl), 1)
    dec_f = jnp.exp(-df_ref[...] * z[0:1, :])
    dec_b = jnp.where(col > 0, jnp.exp(-db_ref[...] * zs[0:1, :]), 0.0)
    c = HY_CH
    o_ref[0 * c:1 * c, :] = (ff[:c] * dec_f).astype(o_ref.dtype)
    o_ref[1 * c:2 * c, :] = (fb[:c] * dec_b).astype(o_ref.dtype)
    o_ref[2 * c:3 * c, :] = (ff[c:] * dec_f).astype(o_ref.dtype)
    o_ref[3 * c:4 * c, :] = (fb[c:] * dec_b).astype(o_ref.dtype)


def _hyena_filters(zt, zst, deltas_f, deltas_b, w1, b1, w2, b2, w3, b3, w4, freq, tl=512):
    seq = zt.shape[1]
    k = HF_ORDER
    w1t = jnp.zeros((k, k), F32).at[:, :HF_EMB].set(w1.astype(F32).T)
    w4r = w4.astype(F32).reshape(k, 2, 2, HY_CH)
    w4ft = w4r[:, :, 0, :].reshape(k, 2 * HY_CH).T
    w4bt = w4r[:, :, 1, :].reshape(k, 2 * HY_CH).T
    colv = lambda a: a.reshape(-1, 1).astype(F32)
    full = lambda shape: pl.BlockSpec(shape, lambda i: (0, 0))
    return pl.pallas_call(
        _filter_kernel,
        grid=(seq // tl,),
        in_specs=[
            pl.BlockSpec((k, tl), lambda i: (0, i)), pl.BlockSpec((k, tl), lambda i: (0, i)),
            full((k, k)), full((k, 1)), full((k, k)), full((k, 1)), full((k, k)), full((k, 1)),
            full((2 * HY_CH, k)), full((2 * HY_CH, k)), full((k, 1)), full((HY_CH, 1)), full((HY_CH, 1)),
        ],
        out_specs=pl.BlockSpec((4 * HY_CH, tl), lambda i: (0, i)),
        out_shape=jax.ShapeDtypeStruct((4 * HY_CH, seq), BF16),
        compiler_params=_cparams(("parallel",)),
        name="hyena_filters",
    )(zt, zst, w1t, colv(b1), w2.astype(F32).T, colv(b2), w3.astype(F32).T, colv(b3), w4ft, w4bt,
      colv(freq), deltas_f, deltas_b)


def _dft_tables():
    k1 = np.arange(FFT_R, dtype=np.int64)[:, None]
    n1 = np.arange(FFT_HALF, dtype=np.int64)[None, :]
    a1 = -2.0 * np.pi * ((k1 * n1) % FFT_R) / FFT_R
    f1 = np.concatenate([np.cos(a1), np.sin(a1)], axis=0)
    f1_inv = np.concatenate([np.cos(a1).T, np.sin(a1).T], axis=1)
    n2 = np.arange(FFT_R, dtype=np.int64)[None, :]
    at = -2.0 * np.pi * ((k1 * n2) % FFT_N) / FFT_N
    k2 = np.arange(FFT_R, dtype=np.int64)[None, :]
    a2 = -2.0 * np.pi * ((n2.T * k2) % FFT_R) / FFT_R
    fr, fi = np.cos(a2), np.sin(a2)
    c2 = np.block([[fr, fi], [-fi, fr]])
    c2_inv = np.block([[fr, -fi], [fi, fr]])
    f1r, f1i = np.cos(a1), np.sin(a1)
    f1c = np.block([[f1r, -f1i], [f1i, f1r]])
    f1c_inv = np.block([[f1r.T, f1i.T], [-f1i.T, f1r.T]])
    as_bf16 = lambda a: jnp.asarray(a, F32).astype(BF16)
    return dict(f1=as_bf16(f1), f1_inv=as_bf16(f1_inv), c2=as_bf16(c2), c2_inv=as_bf16(c2_inv),
                f1c=as_bf16(f1c), f1c_inv=as_bf16(f1c_inv),
                tr=jnp.asarray(np.cos(at), F32), ti=jnp.asarray(np.sin(at), F32))


def _dft_many(xs, f1_ref, c2_ref, tr, ti):
    r = FFT_R
    xx = jnp.concatenate([x.astype(BF16) for x in xs], axis=1)
    a = jnp.dot(f1_ref[...], xx, preferred_element_type=F32)
    rows = []
    for s in range(len(xs)):
        ar, ai = a[:r, s * r:(s + 1) * r], a[r:, s * r:(s + 1) * r]
        rows.append(jnp.concatenate([ar * tr - ai * ti, ar * ti + ai * tr], axis=1).astype(BF16))
    return jnp.dot(jnp.concatenate(rows, axis=0), c2_ref[...], preferred_element_type=F32)


def _idft_many(ys, f1i_ref, c2i_ref, tr, ti):
    r = FFT_R
    y = jnp.concatenate([v.astype(BF16) for v in ys], axis=0)
    b = jnp.dot(y, c2i_ref[...], preferred_element_type=F32)
    outs_r, outs_i = [], []
    for s in range(len(ys)):
        br, bi = b[s * r:(s + 1) * r, :r], b[s * r:(s + 1) * r, r:]
        outs_r.append((br * tr + bi * ti).astype(BF16))
        outs_i.append((bi * tr - br * ti).astype(BF16))
    rhs = jnp.concatenate([jnp.concatenate(outs_r, axis=1), jnp.concatenate(outs_i, axis=1)], axis=0)
    return jnp.dot(f1i_ref[...], rhs, preferred_element_type=F32)


def _filter_spectrum_kernel(hf_ref, hb_ref, f1_ref, c2_ref, tr_ref, ti_ref, o_ref):
    r = FFT_R
    cb = hf_ref.shape[2]
    xs = [ref[0, 0, ci] for ci in range(cb) for ref in (hf_ref, hb_ref)]
    x = _dft_many(xs, f1_ref, c2_ref, tr_ref[...], ti_ref[...])
    for ci in range(cb):
        xf, xb = x[2 * ci * r:(2 * ci + 1) * r], x[(2 * ci + 1) * r:(2 * ci + 2) * r]
        o_ref[0, ci] = jnp.concatenate([xf[:, :r] + xb[:, :r], xf[:, r:] - xb[:, r:]],
                                       axis=1).astype(o_ref.dtype)


def _filter_spectrum(filt5, tabs):
    n_ord, _, n_ch, rows, _ = filt5.shape
    cb = HY_CBLK
    const = lambda shape: pl.BlockSpec(shape, lambda o, c: (0, 0))
    return pl.pallas_call(
        _filter_spectrum_kernel,
        grid=(n_ord, n_ch // cb),
        in_specs=[
            pl.BlockSpec((1, 1, cb, rows, LANES), lambda o, c: (o, 0, c, 0, 0)),
            pl.BlockSpec((1, 1, cb, rows, LANES), lambda o, c: (o, 1, c, 0, 0)),
            const((2 * FFT_R, FFT_HALF)), const((2 * FFT_R, 2 * FFT_R)),
            const((FFT_R, FFT_R)), const((FFT_R, FFT_R)),
        ],
        out_specs=pl.BlockSpec((1, cb, FFT_R, 2 * FFT_R), lambda o, c: (o, c, 0, 0)),
        out_shape=jax.ShapeDtypeStruct((n_ord, n_ch, FFT_R, 2 * FFT_R), BF16),
        compiler_params=_cparams(("parallel", "parallel")),
        name="filter_spectrum",
    )(filt5, filt5, tabs["f1"], tabs["c2"], tabs["tr"], tabs["ti"])


def _short_conv_mat(u, w0, w1, w2, bias):
    rows, lanes = u.shape
    lane = lax.broadcasted_iota(jnp.int32, u.shape, 1)
    row = lax.broadcasted_iota(jnp.int32, u.shape, 0)
    prev = pltpu.roll(u, 1, 1)
    prev = jnp.where(lane == 0, jnp.where(row == 0, 0.0, pltpu.roll(prev, 1, 0)), prev)
    nxt = pltpu.roll(u, lanes - 1, 1)
    nxt = jnp.where(lane == lanes - 1, jnp.where(row == rows - 1, 0.0, pltpu.roll(nxt, rows - 1, 0)), nxt)
    return w0 * prev + w1 * u + w2 * nxt + bias


def _hyena_conv_kernel(cw_ref, d_ref, u_ref, gate_ref, *refs, conv_in, gated_out, u_ch0, gate_ch0, order_ch0):
    if gated_out:
        g2_ref, k_ref, f1_ref, f1i_ref, c2_ref, c2i_ref, tr_ref, ti_ref, o_ref = refs
    else:
        k_ref, f1_ref, f1i_ref, c2_ref, c2i_ref, tr_ref, ti_ref, o_ref = refs
        g2_ref = None
    r = FFT_R
    tr, ti = tr_ref[...], ti_ref[...]
    batch, cb = u_ref.shape[0], u_ref.shape[1]
    c0 = pl.program_id(0) * cb
    signals = [(ci, b) for ci in range(cb) for b in range(batch)]

    def conv_taps(ref, b, ci, ch):
        return _short_conv_mat(ref[b, ci], cw_ref[0, ch], cw_ref[1, ch], cw_ref[2, ch], cw_ref[3, ch])

    def long_conv_input(ci, b):
        return conv_taps(u_ref, b, ci, u_ch0 + c0 + ci) if conv_in else u_ref[b, ci]

    assert batch == 2
    half = FFT_HALF
    us = [long_conv_input(ci, b) for ci, b in signals]
    packed = [jnp.concatenate([us[2 * ci], us[2 * ci + 1]], axis=0) for ci in range(cb)]
    x = _dft_many(packed, f1_ref, c2_ref, tr, ti)
    ys = []
    for ci in range(cb):
        kk = k_ref[0, ci].astype(F32)
        kr, ki = kk[:, :r], kk[:, r:]
        xr, xi = x[ci * r:(ci + 1) * r, :r], x[ci * r:(ci + 1) * r, r:]
        ys.append(jnp.concatenate([xr * kr - xi * ki, xr * ki + xi * kr], axis=1))
    y = _idft_many(ys, f1i_ref, c2i_ref, tr, ti)
    for s, (ci, b) in enumerate(signals):
        gate = conv_taps(gate_ref, b, ci, gate_ch0 + c0 + ci)
        skip = d_ref[0, order_ch0 + c0 + ci] * us[s]
        res = gate * (y[b * half:(b + 1) * half, ci * r:(ci + 1) * r] * (1.0 / FFT_N) + skip)
        if gated_out:
            res = res * _silu(g2_ref[b, ci])
        o_ref[b, ci] = res.astype(o_ref.dtype)


def _hyena_conv(conv_wb, d_skip, u_arr, u_ch0, gate_arr, gate_ch0, g2, spec, order, tabs, out_dtype, conv_in):
    batch = u_arr.shape[0]
    assert batch == 2 and u_arr.shape[2] == FFT_HALF
    cb = HY_CBLK
    smem = pl.BlockSpec(memory_space=pltpu.SMEM)
    chan = lambda ch0: pl.BlockSpec((batch, cb, FFT_HALF, LANES), lambda c: (0, ch0 // cb + c, 0, 0))
    const = lambda shape: pl.BlockSpec(shape, lambda c: (0, 0))
    in_specs = [smem, smem, chan(u_ch0), chan(gate_ch0)]
    args = [conv_wb, d_skip, u_arr, gate_arr]
    if g2 is not None:
        in_specs.append(chan(g2[1]))
        args.append(g2[0])
    in_specs += [
        pl.BlockSpec((1, cb, FFT_R, 2 * FFT_R), lambda c: (order, c, 0, 0)),
        const((2 * FFT_R, FFT_R)), const((FFT_R, 2 * FFT_R)),
        const((2 * FFT_R, 2 * FFT_R)), const((2 * FFT_R, 2 * FFT_R)),
        const((FFT_R, FFT_R)), const((FFT_R, FFT_R)),
    ]
    args += [spec, tabs["f1c"], tabs["f1c_inv"], tabs["c2"], tabs["c2_inv"], tabs["tr"], tabs["ti"]]
    return pl.pallas_call(
        functools.partial(_hyena_conv_kernel, conv_in=conv_in, gated_out=g2 is not None,
                          u_ch0=u_ch0, gate_ch0=gate_ch0, order_ch0=order * HY_CH),
        grid=(HY_CH // cb,),
        in_specs=in_specs,
        out_specs=pl.BlockSpec((batch, cb, FFT_HALF, LANES), lambda c: (0, c, 0, 0)),
        out_shape=jax.ShapeDtypeStruct((batch, HY_CH, FFT_HALF, LANES), out_dtype),
        compiler_params=_cparams(("parallel",)),
        name="hyena_conv_gated" if g2 is not None else "hyena_conv",
    )(*args)


def _out_kernel(ya_ref, ybt_ref, mq_ref, mg_ref, mk_ref, mv_ref, w_ref, x_ref, gf_ref, o_ref, *, final):
    gw = GROUP_W
    acc = x_ref[...]
    acc += jnp.dot(ya_ref[...], w_ref[0:gw, :], preferred_element_type=F32)
    acc += lax.dot_general(ybt_ref[0], w_ref[gw:2 * gw, :], (((0,), (0,)), ((), ())),
                           preferred_element_type=F32)
    hd = MEM_HEAD_DIM
    heads = []
    for h in range(MEM_HEADS):
        sl = slice(h * hd, (h + 1) * hd)
        s = lax.dot_general(mq_ref[:, sl], mk_ref[:, sl], (((1,), (1,)), ((), ())),
                            preferred_element_type=F32) * (hd ** -0.5)
        e = jnp.exp(s - jnp.max(s, axis=-1, keepdims=True))
        p = e / jnp.sum(e, axis=-1, keepdims=True)
        oh = jnp.dot(p.astype(BF16), mv_ref[:, sl], preferred_element_type=F32)
        heads.append((oh * _silu(mg_ref[:, sl])).astype(BF16))
    acc += jnp.dot(jnp.concatenate(heads, axis=1), w_ref[2 * gw:3 * gw, :], preferred_element_type=F32)
    if final:
        acc = _rms_norm_rows(acc, gf_ref[...])
    o_ref[...] = acc


def _out_proj(ya, ybt, vm, feats, mg_blk, mkv, w_out, x, g_final, final, seq, mem_len, tm=512):
    m, d = x.shape
    gw = GROUP_W
    per_batch = seq // tm
    return pl.pallas_call(
        functools.partial(_out_kernel, final=final),
        grid=(m // tm,),
        in_specs=[
            pl.BlockSpec((tm, gw), lambda i: (i, 0)),
            pl.BlockSpec((1, gw, tm), lambda i: (i // per_batch, 0, i % per_batch)),
            pl.BlockSpec((tm, gw), lambda i: (i, 1)),
            pl.BlockSpec((tm, gw), lambda i: (i, mg_blk)),
            pl.BlockSpec((mem_len, gw), lambda i: (i // per_batch, 0)),
            pl.BlockSpec((mem_len, gw), lambda i: (i // per_batch, 1)),
            pl.BlockSpec((3 * gw, d), lambda i: (0, 0)),
            pl.BlockSpec((tm, d), lambda i: (i, 0)),
            pl.BlockSpec((1, d), lambda i: (0, 0)),
        ],
        out_specs=pl.BlockSpec((tm, d), lambda i: (i, 0)),
        out_shape=jax.ShapeDtypeStruct((m, d), F32),
        compiler_params=_cparams(("parallel",)),
        name="out_proj_final" if final else "out_proj",
    )(ya, ybt, vm, feats, mkv, mkv, w_out, x, g_final.reshape(1, d).astype(F32))


def _rope_tables(seq):
    pos = np.arange(seq, dtype=np.float64)
    inv_freq = ROPE_THETA ** (-np.arange(0, DA_QK_DIM, 2, dtype=np.float64) / DA_QK_DIM)
    ang = pos[:, None] * inv_freq[None, :]
    cos, sin = np.cos(ang), np.sin(ang)
    reps = LANES // DA_QK_DIM
    return (jnp.asarray(np.tile(np.concatenate([cos, cos], axis=-1), (1, reps)), F32),
            jnp.asarray(np.tile(np.concatenate([-sin, sin], axis=-1), (1, reps)), F32))


def _filter_features(seq):
    t = np.linspace(0.0, 1.0, seq)[:, None]
    w = 2.0 * math.pi * np.arange(seq, dtype=np.float64)[:, None] / seq
    f = np.linspace(1e-4, HF_BANDS - 1, HF_BANDS)[None, :]
    z = np.concatenate([t, np.cos(f * w), -np.sin(f * w)], axis=-1)
    z = np.pad(z, ((0, 0), (0, HF_ORDER - HF_EMB)))
    zs = np.concatenate([z[:1], z[:-1]], axis=0)
    max_decay = math.log(HF_TARGET) / HF_FAST
    min_decay = math.log(HF_TARGET) / HF_SLOW
    deltas = np.abs(np.linspace(min_decay, max_decay, HY_CH))
    as_f32 = lambda a: jnp.asarray(np.ascontiguousarray(a), F32)
    return as_f32(z.T), as_f32(zs.T), as_f32(deltas[:, None]), as_f32(deltas[::-1][:, None])


def kernel(x, mem, g_norm, w_in, da_lam_q1, da_lam_k1, da_lam_q2, da_lam_k2, da_subln_g, hy_conv_w, hy_conv_b, hf_w1, hf_b1, hf_w2, hf_b2, hf_w3, hf_b3, hf_w4, hf_freq, hy_skip, g_mem, w_mem_kv, w_out, g_final):
    batch, seq, d = x.shape
    assert seq == SEQ_LEN
    mem_len = mem.shape[1]
    depth = w_in.shape[0]
    gw = GROUP_W
    xf = x.reshape(batch * seq, d).astype(F32)
    memf = mem.reshape(batch * mem_len, d).astype(F32)
    rope_tabs = _rope_tables(seq)
    zt, zst, deltas_f, deltas_b = _filter_features(seq)
    tabs = _dft_tables()

    for l in range(depth):
        w = w_in[l].astype(BF16)
        w_tok = jnp.concatenate([w[:, :3 * gw], w[:, 8 * gw:9 * gw], w[:, 3 * gw:4 * gw], w[:, 9 * gw:]], axis=1)
        w_hy_t = w[:, 4 * gw:8 * gw].T
        qk, vm, gates, hy = _input_proj(xf, g_norm[l], w_tok, w_hy_t, rope_tabs,
                                        DA_QK_DIM ** -0.5 * math.log2(math.e),
                                        batch, seq)
        hy = hy.reshape(batch, 4 * HY_CH, FFT_HALF, LANES)
        mkv = _norm_matmul(memf, g_mem[l], w_mem_kv[l].astype(BF16), BF16, "proj_mem_kv")

        lam_init = 0.8 - 0.6 * math.exp(-0.3 * l)
        ya = _diff_attention(qk, vm, gates, da_subln_g[l], da_lam_q1[l], da_lam_k1[l],
                             da_lam_q2[l], da_lam_k2[l], lam_init, batch, seq)

        filt = _hyena_filters(zt, zst, deltas_f, deltas_b, hf_w1[l], hf_b1[l], hf_w2[l], hf_b2[l],
                              hf_w3[l], hf_b3[l], hf_w4[l], hf_freq[l])
        spec = _filter_spectrum(filt.reshape(2, 2, HY_CH, FFT_HALF, LANES), tabs)
        conv_wb = jnp.concatenate([hy_conv_w[l], hy_conv_b[l][None, :]], axis=0).astype(F32)
        d_skip = hy_skip[l].reshape(1, -1).astype(F32)
        z1 = _hyena_conv(conv_wb, d_skip, hy, 0, hy, HY_CH, None, spec, 0, tabs, F32, conv_in=True)
        ybt = _hyena_conv(conv_wb, d_skip, z1, 0, hy, 2 * HY_CH, (hy, 3 * HY_CH), spec, 1, tabs, BF16,
                          conv_in=False)
        ybt = ybt.reshape(batch, HY_CH, seq)

        xf = _out_proj(ya, ybt, vm, gates, 1, mkv, w_out[l].astype(BF16), xf, g_final,
                       l == depth - 1, seq, mem_len)
    return xf.reshape(batch, seq, d)
```

```python
import functools
import math

import jax
import jax.numpy as jnp
import numpy as np
from jax import lax
from jax.experimental import pallas as pl
from jax.experimental.pallas import tpu as pltpu

F32 = jnp.float32
BF16 = jnp.bfloat16

GROUP_W = 512
DA_HEADS = 4
DA_QK_DIM = 64
DA_V_DIM = 128
HY_CH = 512
HF_EMB = 33
HF_BANDS = 16
HF_ORDER = 64
HF_TARGET = 1e-2
HF_FAST = 0.3
HF_SLOW = 1.5
MEM_HEADS = 4
MEM_HEAD_DIM = 128
ROPE_THETA = 10000.0
EPS = 1e-6

LANES = 128
BF16_SUBLANES = 16
VMEM_LIMIT = 56 * 1024 * 1024
ATTN_VMEM_LIMIT = 60 * 1024 * 1024

FFT_R = 128
FFT_N = FFT_R * FFT_R
SEQ_LEN = FFT_N // 2
FFT_HALF = FFT_R // 2
HY_CBLK = 16


def _cparams(sem):
    return pltpu.CompilerParams(dimension_semantics=sem, vmem_limit_bytes=VMEM_LIMIT)


def _silu(g):
    return g * (1.0 / (1.0 + jnp.exp(-g)))


def _rms_norm_rows(x, g):
    ms = jnp.mean(x * x, axis=-1, keepdims=True)
    return x * lax.rsqrt(ms + EPS) * g


def _norm_matmul_kernel(x_ref, g_ref, w_ref, *refs, rope, q_scale):
    if rope:
        cos_ref, sin_ref, o_ref, h_ref = refs
    else:
        o_ref, h_ref = refs
    j = pl.program_id(1)

    @pl.when(j == 0)
    def _():
        h_ref[...] = _rms_norm_rows(x_ref[...], g_ref[...]).astype(BF16)

    acc = jnp.dot(h_ref[...], w_ref[...], preferred_element_type=F32)
    if rope:
        tm, tn = acc.shape
        scale = jnp.where(j == 0, q_scale, 1.0).astype(F32)
        cos = cos_ref[...]
        sin = sin_ref[...]
        lane = lax.broadcasted_iota(jnp.int32, (tm, LANES), 1)
        first_half = (lane % DA_QK_DIM) < (DA_QK_DIM // 2)
        for c in range(tn // LANES):
            a = acc[:, c * LANES:(c + 1) * LANES]
            partner = jnp.where(first_half,
                                pltpu.roll(a, LANES - DA_QK_DIM // 2, 1),
                                pltpu.roll(a, DA_QK_DIM // 2, 1))
            o_ref[:, c * LANES:(c + 1) * LANES] = ((a * cos + partner * sin) * scale).astype(o_ref.dtype)
    else:
        o_ref[...] = acc.astype(o_ref.dtype)


def _norm_matmul(x, g, w, out_dtype, name, rope_tabs=None, q_scale=1.0, tm=1024, tn=512):
    m, d = x.shape
    n = w.shape[1]
    tm = min(tm, m)
    assert m % tm == 0 and n % tn == 0
    in_specs = [
        pl.BlockSpec((tm, d), lambda i, j: (i, 0)),
        pl.BlockSpec((1, d), lambda i, j: (0, 0)),
        pl.BlockSpec((d, tn), lambda i, j: (0, j)),
    ]
    args = [x, g.reshape(1, d), w]
    if rope_tabs is not None:
        n_pos = rope_tabs[0].shape[0] // tm
        for t in rope_tabs:
            in_specs.append(pl.BlockSpec((tm, LANES), lambda i, j: (i % n_pos, 0)))
            args.append(t)
    return pl.pallas_call(
        functools.partial(_norm_matmul_kernel, rope=rope_tabs is not None, q_scale=q_scale),
        grid=(m // tm, n // tn),
        in_specs=in_specs,
        out_specs=pl.BlockSpec((tm, tn), lambda i, j: (i, j)),
        out_shape=jax.ShapeDtypeStruct((m, n), out_dtype),
        scratch_shapes=[pltpu.VMEM((tm, d), BF16)],
        compiler_params=_cparams(("parallel", "arbitrary")),
        name=name,
    )(*args)


PROJ_TN = 1024
PROJ_TOKEN_TILES = 3
PROJ_CHANNEL_TILES = 2


def _input_proj_kernel(x_ref, g_ref, wqk_ref, wv_ref, wmq_ref, wag_ref, wmg_ref, wt_ref, cos_ref, sin_ref,
                       qk_ref, vm_ref, gate_ref, hy_ref, h_ref, *, q_scale):
    j = pl.program_id(1)
    tn = PROJ_TN
    gw = GROUP_W

    @pl.when(j == 0)
    def _():
        h_ref[...] = _rms_norm_rows(x_ref[...], g_ref[...]).astype(BF16)

    def project(w_ref):
        return jnp.dot(h_ref[...], w_ref[...], preferred_element_type=F32)

    @pl.when(j == 0)
    def _():
        acc = project(wqk_ref)
        tm = acc.shape[0]
        cos = cos_ref[...]
        sin = sin_ref[...]
        lane = lax.broadcasted_iota(jnp.int32, (tm, LANES), 1)
        first_half = (lane % DA_QK_DIM) < (DA_QK_DIM // 2)
        for c in range(tn // LANES):
            scale = q_scale if c * LANES < GROUP_W else 1.0
            a = acc[:, c * LANES:(c + 1) * LANES]
            partner = jnp.where(first_half,
                                pltpu.roll(a, LANES - DA_QK_DIM // 2, 1),
                                pltpu.roll(a, DA_QK_DIM // 2, 1))
            qk_ref[:, c * LANES:(c + 1) * LANES] = ((a * cos + partner * sin) * scale).astype(qk_ref.dtype)

    @pl.when(j == 1)
    def _():
        vm_ref[:, :gw] = project(wv_ref).astype(vm_ref.dtype)
        vm_ref[:, gw:] = project(wmq_ref).astype(vm_ref.dtype)

    @pl.when(j == 2)
    def _():
        gate_ref[:, :gw] = project(wag_ref)
        gate_ref[:, gw:] = project(wmg_ref)

    for t in range(PROJ_CHANNEL_TILES):
        @pl.when(j == PROJ_TOKEN_TILES + t)
        def _(t=t):
            hy_ref[0] = lax.dot_general(wt_ref[t * tn:(t + 1) * tn, :], h_ref[...], (((1,), (1,)), ((), ())),
                                        preferred_element_type=F32)


def _input_proj(x, g, w, w_ch_t, rope_tabs, q_scale, batch, seq, tm=1024):
    m, d = x.shape
    tn = PROJ_TN
    gw = GROUP_W
    assert w.shape == (d, 10 * gw) and w_ch_t.shape == (PROJ_CHANNEL_TILES * tn, d) and tn == 2 * gw
    per_batch = seq // tm
    once = pl.Buffered(1)
    clip = lambda j, lo, n: jnp.clip(j - lo, 0, n - 1)
    w_cols = lambda width, blk: pl.BlockSpec((d, width), lambda i, j: (0, blk), pipeline_mode=once)
    return pl.pallas_call(
        functools.partial(_input_proj_kernel, q_scale=q_scale),
        grid=(m // tm, PROJ_TOKEN_TILES + PROJ_CHANNEL_TILES),
        in_specs=[
            pl.BlockSpec((tm, d), lambda i, j: (i, 0)),
            pl.BlockSpec((1, d), lambda i, j: (0, 0)),
            w_cols(2 * gw, 0), w_cols(gw, 2), w_cols(gw, 8), w_cols(gw, 3), w_cols(gw, 9),
            pl.BlockSpec(w_ch_t.shape, lambda i, j: (0, 0), pipeline_mode=once),
            pl.BlockSpec((tm, LANES), lambda i, j: (i % per_batch, 0)),
            pl.BlockSpec((tm, LANES), lambda i, j: (i % per_batch, 0)),
        ],
        out_specs=[
            pl.BlockSpec((tm, tn), lambda i, j: (i, 0)),
            pl.BlockSpec((tm, tn), lambda i, j: (i, 0)),
            pl.BlockSpec((tm, tn), lambda i, j: (i, 0)),
            pl.BlockSpec((1, tn, tm), lambda i, j: (i // per_batch, clip(j, PROJ_TOKEN_TILES, PROJ_CHANNEL_TILES),
                                                    i % per_batch)),
        ],
        out_shape=[
            jax.ShapeDtypeStruct((m, tn), BF16),
            jax.ShapeDtypeStruct((m, tn), BF16),
            jax.ShapeDtypeStruct((m, tn), F32),
            jax.ShapeDtypeStruct((batch, PROJ_CHANNEL_TILES * tn, seq), F32),
        ],
        scratch_shapes=[pltpu.VMEM((tm, d), BF16)],
        compiler_params=_cparams(("parallel", "arbitrary")),
        name="input_proj",
    )(x, g.reshape(1, d), w, w, w, w, w, w_ch_t, rope_tabs[0], rope_tabs[1])


def _diff_attn_kernel(q_ref, k_ref, v_ref, gate_ref, gsub_ref, lq1_ref, lk1_ref, lq2_ref, lk2_ref,
                      o_ref, vt_ref, q2_ref, m_ref, acc_ref, accf_ref, s_a, s_b, cm_a, cm_b, p_a, p_b,
                      *, lam_init, tq, tk, t_chunk):
    seq = k_ref.shape[0]
    dv = DA_V_DIM
    n_tiles = seq // tq
    n_chunks = seq // tk
    assert n_chunks % 2 == 0 and n_chunks >= 4

    row = lax.broadcasted_iota(jnp.int32, (BF16_SUBLANES, t_chunk), 0)
    ones_row = jnp.where(row == 0, 1.0, 0.0).astype(BF16)
    for c in range(seq // t_chunk):
        blk = v_ref[c * t_chunk:(c + 1) * t_chunk, :].astype(F32)
        vt_ref[0:dv, c * t_chunk:(c + 1) * t_chunk] = blk.T.astype(BF16)
        vt_ref[dv:dv + BF16_SUBLANES, c * t_chunk:(c + 1) * t_chunk] = ones_row

    lam = (jnp.exp(jnp.sum(lq1_ref[...] * lk1_ref[...], axis=-1, keepdims=True))
           - jnp.exp(jnp.sum(lq2_ref[...] * lk2_ref[...], axis=-1, keepdims=True)) + lam_init)

    def tile_rows(i):
        return pl.ds(pl.multiple_of(i * tq, tq), tq)

    def load_queries(i):
        q = q_ref[tile_rows(i), :]
        lane = lax.broadcasted_iota(jnp.int32, q.shape, 1)
        zero = jnp.zeros_like(q)
        q2_ref[...] = jnp.concatenate([jnp.where(lane < DA_QK_DIM, q, zero),
                                       jnp.where(lane >= DA_QK_DIM, q, zero)], axis=0)

    def finish_tile(i):
        for g in range(tq // LANES):
            rows = pl.ds(pl.multiple_of(i * tq + g * LANES, LANES), LANES)
            c0 = slice(g * LANES, (g + 1) * LANES)
            c1 = slice(tq + g * LANES, tq + (g + 1) * LANES)
            ot = (accf_ref[0:dv, c0] / accf_ref[dv:dv + 1, c0]
                  - lam * (accf_ref[0:dv, c1] / accf_ref[dv:dv + 1, c1]))
            y = _rms_norm_rows(ot.T, gsub_ref[...]) * (1.0 - lam_init)
            o_ref[rows, :] = (y * _silu(gate_ref[rows, :])).astype(o_ref.dtype)

    def scores(c, s_ref, cm_ref):
        kc = k_ref[pl.ds(pl.multiple_of(c * tk, tk), tk), :]
        s = lax.dot_general(kc, q2_ref[...], (((1,), (1,)), ((), ())), preferred_element_type=F32)
        s_ref[...] = s
        cm_ref[...] = jnp.max(s, axis=0, keepdims=True)

    def weighted_values(c, p_ref):
        vtc = vt_ref[:, pl.ds(pl.multiple_of(c * tk, tk), tk)]
        return jnp.dot(vtc, p_ref[...], preferred_element_type=F32)

    def softmax_chunk(s_ref, cm_ref, p_ref):
        m_prev = m_ref[...]
        m_new = jnp.maximum(m_prev, cm_ref[...])
        p_ref[...] = jnp.exp2(s_ref[...] - m_new).astype(BF16)
        m_ref[...] = m_new
        return jnp.exp2(m_prev - m_new)

    def step(c, s_cur, cm_cur, s_next, cm_next, p_cur, p_prev):
        scores(c + 1, s_next, cm_next)
        pending = weighted_values(c - 1, p_prev)
        alpha = softmax_chunk(s_cur, cm_cur, p_cur)
        acc_ref[...] = (acc_ref[...] + pending) * alpha

    acc_ref[...] = jnp.zeros(acc_ref.shape, F32)
    p_b[...] = jnp.zeros(p_b.shape, BF16)
    load_queries(0)
    scores(0, s_a, cm_a)

    even = (s_a, cm_a, s_b, cm_b, p_a, p_b)
    odd = (s_b, cm_b, s_a, cm_a, p_b, p_a)

    def tile(i, carry):
        scores(1, s_b, cm_b)
        accf_ref[...] = acc_ref[...] + weighted_values(n_chunks - 1, p_b)
        m_ref[...] = jnp.full(m_ref.shape, -1e30, F32)
        softmax_chunk(s_a, cm_a, p_a)
        acc_ref[...] = jnp.zeros(acc_ref.shape, F32)
        step(1, *odd)

        @pl.when(i > 0)
        def _():
            finish_tile(i - 1)

        def pair(j, carry):
            c = 2 + 2 * j
            step(c, *even)
            step(c + 1, *odd)
            return carry
        lax.fori_loop(0, (n_chunks - 4) // 2, pair, 0)

        step(n_chunks - 2, *even)
        load_queries(jnp.minimum(i + 1, n_tiles - 1))
        scores(0, s_a, cm_a)
        pending = weighted_values(n_chunks - 2, p_a)
        alpha = softmax_chunk(s_b, cm_b, p_b)
        acc_ref[...] = (acc_ref[...] + pending) * alpha
        return carry
    lax.fori_loop(0, n_tiles, tile, 0)

    accf_ref[...] = acc_ref[...] + weighted_values(n_chunks - 1, p_b)
    finish_tile(n_tiles - 1)


def _diff_attention(qk, vm, gates, g_sub, lq1, lk1, lq2, lk2, lam_init, batch, seq, tq=2048, tk=512):
    head_blocks = GROUP_W // LANES
    vec = lambda a: a.reshape(1, -1).astype(F32)
    small = lambda n: pl.BlockSpec((1, n), lambda b, h: (0, 0))
    once = pl.Buffered(1)
    return pl.pallas_call(
        functools.partial(_diff_attn_kernel, lam_init=lam_init, tq=tq, tk=tk, t_chunk=512),
        grid=(batch, DA_HEADS),
        in_specs=[
            pl.BlockSpec((seq, LANES), lambda b, h: (b, h), pipeline_mode=once),
            pl.BlockSpec((seq, LANES), lambda b, h: (b, head_blocks + h)),
            pl.BlockSpec((seq, LANES), lambda b, h: (b, h)),
            pl.BlockSpec((seq, LANES), lambda b, h: (b, h), pipeline_mode=once),
            small(DA_V_DIM), small(DA_QK_DIM), small(DA_QK_DIM), small(DA_QK_DIM), small(DA_QK_DIM),
        ],
        out_specs=pl.BlockSpec((seq, LANES), lambda b, h: (b, h)),
        out_shape=jax.ShapeDtypeStruct((batch * seq, GROUP_W), BF16),
        scratch_shapes=[
            pltpu.VMEM((DA_V_DIM + BF16_SUBLANES, seq), BF16),
            pltpu.VMEM((2 * tq, LANES), BF16),
            pltpu.VMEM((1, 2 * tq), F32),
            pltpu.VMEM((DA_V_DIM + BF16_SUBLANES, 2 * tq), F32),
            pltpu.VMEM((DA_V_DIM + BF16_SUBLANES, 2 * tq), F32),
            pltpu.VMEM((tk, 2 * tq), F32), pltpu.VMEM((tk, 2 * tq), F32),
            pltpu.VMEM((1, 2 * tq), F32), pltpu.VMEM((1, 2 * tq), F32),
            pltpu.VMEM((tk, 2 * tq), BF16), pltpu.VMEM((tk, 2 * tq), BF16),
        ],
        compiler_params=pltpu.CompilerParams(dimension_semantics=("parallel", "parallel"),
                                             vmem_limit_bytes=ATTN_VMEM_LIMIT),
        name="diff_attention",
    )(qk, qk, vm, gates, vec(g_sub), vec(lq1), vec(lk1), vec(lq2), vec(lk2))


def _filter_kernel(z_ref, w1_ref, b1_ref, w2_ref, b2_ref, w3_ref, b3_ref, w4_ref, fr_ref, delta_ref, o_ref):
    hp = lax.Precision.HIGHEST
    fr = fr_ref[...]

    def dot_split(a, b):
        a_hi = a.astype(BF16)
        a_lo = (a - a_hi.astype(F32)).astype(BF16)
        b_hi = b.astype(BF16)
        b_lo = (b - b_hi.astype(F32)).astype(BF16)
        return (jnp.dot(a_hi, b_hi, preferred_element_type=F32)
                + (jnp.dot(a_hi, b_lo, preferred_element_type=F32)
                   + jnp.dot(a_lo, b_hi, preferred_element_type=F32)))

    z = z_ref[...]
    h = jnp.sin(fr * (jnp.dot(w1_ref[...], z, precision=hp, preferred_element_type=F32) + b1_ref[...]))
    h = jnp.sin(fr * (jnp.dot(w2_ref[...], h, precision=hp, preferred_element_type=F32) + b2_ref[...]))
    h = jnp.sin(fr * (jnp.dot(w3_ref[...], h, precision=hp, preferred_element_type=F32) + b3_ref[...]))
    decay = jnp.exp(-delta_ref[...] * z[0:1, :])
    o_ref[...] = (dot_split(w4_ref[...], h) * decay).astype(o_ref.dtype)


def _hyena_filters(zt, deltas, w1, b1, w2, b2, w3, b3, w4, freq, tl=512):
    seq = zt.shape[1]
    k = HF_ORDER
    n_out = w4.shape[1]
    w1t = jnp.zeros((k, k), F32).at[:, :HF_EMB].set(w1.astype(F32).T)
    colv = lambda a: a.reshape(-1, 1).astype(F32)
    full = lambda shape: pl.BlockSpec(shape, lambda i: (0, 0))
    return pl.pallas_call(
        _filter_kernel,
        grid=(seq // tl,),
        in_specs=[
            pl.BlockSpec((k, tl), lambda i: (0, i)),
            full((k, k)), full((k, 1)), full((k, k)), full((k, 1)), full((k, k)), full((k, 1)),
            full((n_out, k)), full((k, 1)), full((n_out, 1)),
        ],
        out_specs=pl.BlockSpec((n_out, tl), lambda i: (0, i)),
        out_shape=jax.ShapeDtypeStruct((n_out, seq), BF16),
        compiler_params=_cparams(("parallel",)),
        name="hyena_filters",
    )(zt, w1t, colv(b1), w2.astype(F32).T, colv(b2), w3.astype(F32).T, colv(b3), w4.astype(F32).T,
      colv(freq), deltas)


def _dft_tables():
    k1 = np.arange(FFT_R, dtype=np.int64)[:, None]
    n1 = np.arange(FFT_HALF, dtype=np.int64)[None, :]
    a1 = -2.0 * np.pi * ((k1 * n1) % FFT_R) / FFT_R
    f1 = np.concatenate([np.cos(a1), np.sin(a1)], axis=0)
    f1_inv = np.concatenate([np.cos(a1).T, np.sin(a1).T], axis=1)
    n2 = np.arange(FFT_R, dtype=np.int64)[None, :]
    at = -2.0 * np.pi * ((k1 * n2) % FFT_N) / FFT_N
    k2 = np.arange(FFT_R, dtype=np.int64)[None, :]
    a2 = -2.0 * np.pi * ((n2.T * k2) % FFT_R) / FFT_R
    fr, fi = np.cos(a2), np.sin(a2)
    c2 = np.block([[fr, fi], [-fi, fr]])
    c2_inv = np.block([[fr, -fi], [fi, fr]])
    f1r, f1i = np.cos(a1), np.sin(a1)
    f1c = np.block([[f1r, -f1i], [f1i, f1r]])
    f1c_inv = np.block([[f1r.T, f1i.T], [-f1i.T, f1r.T]])
    as_bf16 = lambda a: jnp.asarray(a, F32).astype(BF16)
    return dict(f1=as_bf16(f1), f1_inv=as_bf16(f1_inv), c2=as_bf16(c2), c2_inv=as_bf16(c2_inv),
                f1c=as_bf16(f1c), f1c_inv=as_bf16(f1c_inv),
                tr=jnp.asarray(np.cos(at), F32), ti=jnp.asarray(np.sin(at), F32))


def _dft_many(xs, f1_ref, c2_ref, tr, ti):
    r = FFT_R
    xx = jnp.concatenate([x.astype(BF16) for x in xs], axis=1)
    a = jnp.dot(f1_ref[...], xx, preferred_element_type=F32)
    rows = []
    for s in range(len(xs)):
        ar, ai = a[:r, s * r:(s + 1) * r], a[r:, s * r:(s + 1) * r]
        rows.append(jnp.concatenate([ar * tr - ai * ti, ar * ti + ai * tr], axis=1).astype(BF16))
    return jnp.dot(jnp.concatenate(rows, axis=0), c2_ref[...], preferred_element_type=F32)


def _idft_many(ys, f1i_ref, c2i_ref, tr, ti):
    r = FFT_R
    y = jnp.concatenate([v.astype(BF16) for v in ys], axis=0)
    b = jnp.dot(y, c2i_ref[...], preferred_element_type=F32)
    outs_r, outs_i = [], []
    for s in range(len(ys)):
        br, bi = b[s * r:(s + 1) * r, :r], b[s * r:(s + 1) * r, r:]
        outs_r.append((br * tr + bi * ti).astype(BF16))
        outs_i.append((bi * tr - br * ti).astype(BF16))
    rhs = jnp.concatenate([jnp.concatenate(outs_r, axis=1), jnp.concatenate(outs_i, axis=1)], axis=0)
    return jnp.dot(f1i_ref[...], rhs, preferred_element_type=F32)


def _delay_one(u):
    lane = lax.broadcasted_iota(jnp.int32, u.shape, 1)
    row = lax.broadcasted_iota(jnp.int32, u.shape, 0)
    prev = pltpu.roll(u, 1, 1)
    return jnp.where(lane == 0, jnp.where(row == 0, 0.0, pltpu.roll(prev, 1, 0)), prev)


def _filter_spectrum_kernel(hf_ref, hb_ref, f1_ref, c2_ref, tr_ref, ti_ref, o_ref):
    r = FFT_R
    cb = hf_ref.shape[2]
    xs = []
    for ci in range(cb):
        xs.append(hf_ref[0, 0, ci])
        xs.append(_delay_one(hb_ref[0, 0, ci].astype(F32)))
    x = _dft_many(xs, f1_ref, c2_ref, tr_ref[...], ti_ref[...])
    for ci in range(cb):
        xf, xb = x[2 * ci * r:(2 * ci + 1) * r], x[(2 * ci + 1) * r:(2 * ci + 2) * r]
        o_ref[0, ci] = jnp.concatenate([xf[:, :r] + xb[:, :r], xf[:, r:] - xb[:, r:]],
                                       axis=1).astype(o_ref.dtype)


def _filter_spectrum(filt5, tabs):
    n_ord, _, n_ch, rows, _ = filt5.shape
    cb = HY_CBLK
    const = lambda shape: pl.BlockSpec(shape, lambda o, c: (0, 0))
    return pl.pallas_call(
        _filter_spectrum_kernel,
        grid=(n_ord, n_ch // cb),
        in_specs=[
            pl.BlockSpec((1, 1, cb, rows, LANES), lambda o, c: (o, 0, c, 0, 0)),
            pl.BlockSpec((1, 1, cb, rows, LANES), lambda o, c: (o, 1, c, 0, 0)),
            const((2 * FFT_R, FFT_HALF)), const((2 * FFT_R, 2 * FFT_R)),
            const((FFT_R, FFT_R)), const((FFT_R, FFT_R)),
        ],
        out_specs=pl.BlockSpec((1, cb, FFT_R, 2 * FFT_R), lambda o, c: (o, c, 0, 0)),
        out_shape=jax.ShapeDtypeStruct((n_ord, n_ch, FFT_R, 2 * FFT_R), BF16),
        compiler_params=_cparams(("parallel", "parallel")),
        name="filter_spectrum",
    )(filt5, filt5, tabs["f1"], tabs["c2"], tabs["tr"], tabs["ti"])


def _short_conv_mat(u, w0, w1, w2, bias):
    rows, lanes = u.shape
    lane = lax.broadcasted_iota(jnp.int32, u.shape, 1)
    row = lax.broadcasted_iota(jnp.int32, u.shape, 0)
    prev = _delay_one(u)
    nxt = pltpu.roll(u, lanes - 1, 1)
    nxt = jnp.where(lane == lanes - 1, jnp.where(row == rows - 1, 0.0, pltpu.roll(nxt, rows - 1, 0)), nxt)
    return w0 * prev + w1 * u + w2 * nxt + bias


def _hyena_conv_kernel(cw_ref, d_ref, u_ref, gate_ref, *refs, conv_in, gated_out, u_ch0, gate_ch0, order_ch0):
    if gated_out:
        g2_ref, k_ref, f1_ref, f1i_ref, c2_ref, c2i_ref, tr_ref, ti_ref, o_ref = refs
    else:
        k_ref, f1_ref, f1i_ref, c2_ref, c2i_ref, tr_ref, ti_ref, o_ref = refs
        g2_ref = None
    r = FFT_R
    tr, ti = tr_ref[...], ti_ref[...]
    batch, cb = u_ref.shape[0], u_ref.shape[1]
    c0 = pl.program_id(0) * cb
    signals = [(ci, b) for ci in range(cb) for b in range(batch)]

    def conv_taps(ref, b, ci, ch):
        return _short_conv_mat(ref[b, ci], cw_ref[0, ch], cw_ref[1, ch], cw_ref[2, ch], cw_ref[3, ch])

    def long_conv_input(ci, b):
        return conv_taps(u_ref, b, ci, u_ch0 + c0 + ci) if conv_in else u_ref[b, ci]

    assert batch == 2
    half = FFT_HALF
    us = [long_conv_input(ci, b) for ci, b in signals]
    packed = [jnp.concatenate([us[2 * ci], us[2 * ci + 1]], axis=0) for ci in range(cb)]
    x = _dft_many(packed, f1_ref, c2_ref, tr, ti)
    ys = []
    for ci in range(cb):
        kk = k_ref[0, ci].astype(F32)
        kr, ki = kk[:, :r], kk[:, r:]
        xr, xi = x[ci * r:(ci + 1) * r, :r], x[ci * r:(ci + 1) * r, r:]
        ys.append(jnp.concatenate([xr * kr - xi * ki, xr * ki + xi * kr], axis=1))
    y = _idft_many(ys, f1i_ref, c2i_ref, tr, ti)
    for s, (ci, b) in enumerate(signals):
        gate = conv_taps(gate_ref, b, ci, gate_ch0 + c0 + ci)
        skip = d_ref[0, order_ch0 + c0 + ci] * us[s]
        res = gate * (y[b * half:(b + 1) * half, ci * r:(ci + 1) * r] * (1.0 / FFT_N) + skip)
        if gated_out:
            res = res * _silu(g2_ref[b, ci])
        o_ref[b, ci] = res.astype(o_ref.dtype)


def _hyena_conv(conv_wb, d_skip, u_arr, u_ch0, gate_arr, gate_ch0, g2, spec, order, tabs, out_dtype, conv_in):
    batch = u_arr.shape[0]
    assert batch == 2 and u_arr.shape[2] == FFT_HALF
    cb = HY_CBLK
    smem = pl.BlockSpec(memory_space=pltpu.SMEM)
    chan = lambda ch0: pl.BlockSpec((batch, cb, FFT_HALF, LANES), lambda c: (0, ch0 // cb + c, 0, 0))
    const = lambda shape: pl.BlockSpec(shape, lambda c: (0, 0))
    in_specs = [smem, smem, chan(u_ch0), chan(gate_ch0)]
    args = [conv_wb, d_skip, u_arr, gate_arr]
    if g2 is not None:
        in_specs.append(chan(g2[1]))
        args.append(g2[0])
    in_specs += [
        pl.BlockSpec((1, cb, FFT_R, 2 * FFT_R), lambda c: (order, c, 0, 0)),
        const((2 * FFT_R, FFT_R)), const((FFT_R, 2 * FFT_R)),
        const((2 * FFT_R, 2 * FFT_R)), const((2 * FFT_R, 2 * FFT_R)),
        const((FFT_R, FFT_R)), const((FFT_R, FFT_R)),
    ]
    args += [spec, tabs["f1c"], tabs["f1c_inv"], tabs["c2"], tabs["c2_inv"], tabs["tr"], tabs["ti"]]
    return pl.pallas_call(
        functools.partial(_hyena_conv_kernel, conv_in=conv_in, gated_out=g2 is not None,
                          u_ch0=u_ch0, gate_ch0=gate_ch0, order_ch0=order * HY_CH),
        grid=(HY_CH // cb,),
        in_specs=in_specs,
        out_specs=pl.BlockSpec((batch, cb, FFT_HALF, LANES), lambda c: (0, c, 0, 0)),
        out_shape=jax.ShapeDtypeStruct((batch, HY_CH, FFT_HALF, LANES), out_dtype),
        compiler_params=_cparams(("parallel",)),
        name="hyena_conv_gated" if g2 is not None else "hyena_conv",
    )(*args)


def _out_kernel(ya_ref, ybt_ref, mq_ref, mg_ref, mk_ref, mv_ref, w_ref, x_ref, gf_ref, o_ref, *, final):
    gw = GROUP_W
    acc = x_ref[...]
    acc += jnp.dot(ya_ref[...], w_ref[0:gw, :], preferred_element_type=F32)
    acc += lax.dot_general(ybt_ref[0], w_ref[gw:2 * gw, :], (((0,), (0,)), ((), ())),
                           preferred_element_type=F32)
    hd = MEM_HEAD_DIM
    heads = []
    for h in range(MEM_HEADS):
        sl = slice(h * hd, (h + 1) * hd)
        s = lax.dot_general(mq_ref[:, sl], mk_ref[:, sl], (((1,), (1,)), ((), ())),
                            preferred_element_type=F32) * (hd ** -0.5)
        e = jnp.exp(s - jnp.max(s, axis=-1, keepdims=True))
        p = e / jnp.sum(e, axis=-1, keepdims=True)
        oh = jnp.dot(p.astype(BF16), mv_ref[:, sl], preferred_element_type=F32)
        heads.append((oh * _silu(mg_ref[:, sl])).astype(BF16))
    acc += jnp.dot(jnp.concatenate(heads, axis=1), w_ref[2 * gw:3 * gw, :], preferred_element_type=F32)
    if final:
        acc = _rms_norm_rows(acc, gf_ref[...])
    o_ref[...] = acc


def _out_proj(ya, ybt, vm, feats, mg_blk, mkv, w_out, x, g_final, final, seq, mem_len, tm=512):
    m, d = x.shape
    gw = GROUP_W
    per_batch = seq // tm
    return pl.pallas_call(
        functools.partial(_out_kernel, final=final),
        grid=(m // tm,),
        in_specs=[
            pl.BlockSpec((tm, gw), lambda i: (i, 0)),
            pl.BlockSpec((1, gw, tm), lambda i: (i // per_batch, 0, i % per_batch)),
            pl.BlockSpec((tm, gw), lambda i: (i, 1)),
            pl.BlockSpec((tm, gw), lambda i: (i, mg_blk)),
            pl.BlockSpec((mem_len, gw), lambda i: (i // per_batch, 0)),
            pl.BlockSpec((mem_len, gw), lambda i: (i // per_batch, 1)),
            pl.BlockSpec((3 * gw, d), lambda i: (0, 0)),
            pl.BlockSpec((tm, d), lambda i: (i, 0)),
            pl.BlockSpec((1, d), lambda i: (0, 0)),
        ],
        out_specs=pl.BlockSpec((tm, d), lambda i: (i, 0)),
        out_shape=jax.ShapeDtypeStruct((m, d), F32),
        compiler_params=_cparams(("parallel",)),
        name="out_proj_final" if final else "out_proj",
    )(ya, ybt, vm, feats, mkv, mkv, w_out, x, g_final.reshape(1, d).astype(F32))


def _rope_tables(seq):
    pos = np.arange(seq, dtype=np.float64)
    inv_freq = ROPE_THETA ** (-np.arange(0, DA_QK_DIM, 2, dtype=np.float64) / DA_QK_DIM)
    ang = pos[:, None] * inv_freq[None, :]
    cos, sin = np.cos(ang), np.sin(ang)
    reps = LANES // DA_QK_DIM
    return (jnp.asarray(np.tile(np.concatenate([cos, cos], axis=-1), (1, reps)), F32),
            jnp.asarray(np.tile(np.concatenate([-sin, sin], axis=-1), (1, reps)), F32))


def _filter_features(seq):
    t = np.linspace(0.0, 1.0, seq)[:, None]
    w = 2.0 * math.pi * np.arange(seq, dtype=np.float64)[:, None] / seq
    f = np.linspace(1e-4, HF_BANDS - 1, HF_BANDS)[None, :]
    z = np.concatenate([t, np.cos(f * w), -np.sin(f * w)], axis=-1)
    z = np.pad(z, ((0, 0), (0, HF_ORDER - HF_EMB)))
    max_decay = math.log(HF_TARGET) / HF_FAST
    min_decay = math.log(HF_TARGET) / HF_SLOW
    deltas = np.abs(np.linspace(min_decay, max_decay, HY_CH))
    per_order = np.concatenate([deltas, deltas[::-1]])
    as_f32 = lambda a: jnp.asarray(np.ascontiguousarray(a), F32)
    return as_f32(z.T), as_f32(np.tile(per_order, 2)[:, None])


def kernel(x, mem, g_norm, w_in, da_lam_q1, da_lam_k1, da_lam_q2, da_lam_k2, da_subln_g, hy_conv_w, hy_conv_b, hf_w1, hf_b1, hf_w2, hf_b2, hf_w3, hf_b3, hf_w4, hf_freq, hy_skip, g_mem, w_mem_kv, w_out, g_final):
    batch, seq, d = x.shape
    assert seq == SEQ_LEN
    mem_len = mem.shape[1]
    depth = w_in.shape[0]
    gw = GROUP_W
    xf = x.reshape(batch * seq, d).astype(F32)
    memf = mem.reshape(batch * mem_len, d).astype(F32)
    rope_tabs = _rope_tables(seq)
    zt, deltas = _filter_features(seq)
    tabs = _dft_tables()

    for l in range(depth):
        w = w_in[l].astype(BF16)
        w_hy_t = w[:, 4 * gw:8 * gw].T
        qk, vm, gates, hy = _input_proj(xf, g_norm[l], w, w_hy_t, rope_tabs,
                                        DA_QK_DIM ** -0.5 * math.log2(math.e),
                                        batch, seq)
        hy = hy.reshape(batch, 4 * HY_CH, FFT_HALF, LANES)
        mkv = _norm_matmul(memf, g_mem[l], w_mem_kv[l].astype(BF16), BF16, "proj_mem_kv")

        lam_init = 0.8 - 0.6 * math.exp(-0.3 * l)
        ya = _diff_attention(qk, vm, gates, da_subln_g[l], da_lam_q1[l], da_lam_k1[l],
                             da_lam_q2[l], da_lam_k2[l], lam_init, batch, seq)

        filt = _hyena_filters(zt, deltas, hf_w1[l], hf_b1[l], hf_w2[l], hf_b2[l],
                              hf_w3[l], hf_b3[l], hf_w4[l], hf_freq[l])
        spec = _filter_spectrum(filt.reshape(2, 2, HY_CH, FFT_HALF, LANES), tabs)
        conv_wb = jnp.concatenate([hy_conv_w[l], hy_conv_b[l][None, :]], axis=0).astype(F32)
        d_skip = hy_skip[l].reshape(1, -1).astype(F32)
        z1 = _hyena_conv(conv_wb, d_skip, hy, 0, hy, HY_CH, None, spec, 0, tabs, F32, conv_in=True)
        ybt = _hyena_conv(conv_wb, d_skip, z1, 0, hy, 2 * HY_CH, (hy, 3 * HY_CH), spec, 1, tabs, BF16,
                          conv_in=False)
        ybt = ybt.reshape(batch, HY_CH, seq)

        xf = _out_proj(ya, ybt, vm, gates, 1, mkv, w_out[l].astype(BF16), xf, g_final,
                       l == depth - 1, seq, mem_len)
    return xf.reshape(batch, seq, d)
```

```python
import functools
import math

import jax
import jax.numpy as jnp
import numpy as np
from jax import lax
from jax.experimental import pallas as pl
from jax.experimental.pallas import tpu as pltpu

F32 = jnp.float32
BF16 = jnp.bfloat16

GROUP_W = 512
DA_HEADS = 4
DA_QK_DIM = 64
DA_V_DIM = 128
HY_CH = 512
HF_EMB = 33
HF_BANDS = 16
HF_ORDER = 64
HF_TARGET = 1e-2
HF_FAST = 0.3
HF_SLOW = 1.5
MEM_HEADS = 4
MEM_HEAD_DIM = 128
ROPE_THETA = 10000.0
EPS = 1e-6

LANES = 128
BF16_SUBLANES = 16
VMEM_LIMIT = 56 * 1024 * 1024
ATTN_VMEM_LIMIT = 60 * 1024 * 1024

FFT_R = 128
FFT_N = FFT_R * FFT_R
SEQ_LEN = FFT_N // 2
FFT_HALF = FFT_R // 2
HY_CBLK = 16


def _cparams(sem):
    return pltpu.CompilerParams(dimension_semantics=sem, vmem_limit_bytes=VMEM_LIMIT)


def _silu(g):
    return g * (1.0 / (1.0 + jnp.exp(-g)))


def _rms_norm_rows(x, g):
    ms = jnp.mean(x * x, axis=-1, keepdims=True)
    return x * lax.rsqrt(ms + EPS) * g


def _norm_matmul_kernel(x_ref, g_ref, w_ref, *refs, rope, q_scale):
    if rope:
        cos_ref, sin_ref, o_ref, h_ref = refs
    else:
        o_ref, h_ref = refs
    j = pl.program_id(1)

    @pl.when(j == 0)
    def _():
        h_ref[...] = _rms_norm_rows(x_ref[...], g_ref[...]).astype(BF16)

    acc = jnp.dot(h_ref[...], w_ref[...], preferred_element_type=F32)
    if rope:
        tm, tn = acc.shape
        scale = jnp.where(j == 0, q_scale, 1.0).astype(F32)
        cos = cos_ref[...]
        sin = sin_ref[...]
        lane = lax.broadcasted_iota(jnp.int32, (tm, LANES), 1)
        first_half = (lane % DA_QK_DIM) < (DA_QK_DIM // 2)
        for c in range(tn // LANES):
            a = acc[:, c * LANES:(c + 1) * LANES]
            partner = jnp.where(first_half,
                                pltpu.roll(a, LANES - DA_QK_DIM // 2, 1),
                                pltpu.roll(a, DA_QK_DIM // 2, 1))
            o_ref[:, c * LANES:(c + 1) * LANES] = ((a * cos + partner * sin) * scale).astype(o_ref.dtype)
    else:
        o_ref[...] = acc.astype(o_ref.dtype)


def _norm_matmul(x, g, w, out_dtype, name, rope_tabs=None, q_scale=1.0, tm=1024, tn=512):
    m, d = x.shape
    n = w.shape[1]
    tm = min(tm, m)
    assert m % tm == 0 and n % tn == 0
    in_specs = [
        pl.BlockSpec((tm, d), lambda i, j: (i, 0)),
        pl.BlockSpec((1, d), lambda i, j: (0, 0)),
        pl.BlockSpec((d, tn), lambda i, j: (0, j)),
    ]
    args = [x, g.reshape(1, d), w]
    if rope_tabs is not None:
        n_pos = rope_tabs[0].shape[0] // tm
        for t in rope_tabs:
            in_specs.append(pl.BlockSpec((tm, LANES), lambda i, j: (i % n_pos, 0)))
            args.append(t)
    return pl.pallas_call(
        functools.partial(_norm_matmul_kernel, rope=rope_tabs is not None, q_scale=q_scale),
        grid=(m // tm, n // tn),
        in_specs=in_specs,
        out_specs=pl.BlockSpec((tm, tn), lambda i, j: (i, j)),
        out_shape=jax.ShapeDtypeStruct((m, n), out_dtype),
        scratch_shapes=[pltpu.VMEM((tm, d), BF16)],
        compiler_params=_cparams(("parallel", "arbitrary")),
        name=name,
    )(*args)


PROJ_TN = 1024
PROJ_TOKEN_TILES = 3
PROJ_CHANNEL_TILES = 2


def _input_proj_kernel(x_ref, g_ref, wqk_ref, wv_ref, wmq_ref, wag_ref, wmg_ref, wt_ref, cos_ref, sin_ref,
                       qk_ref, vm_ref, gate_ref, hy_ref, h_ref, *, q_scale):
    j = pl.program_id(1)
    tn = PROJ_TN
    gw = GROUP_W

    @pl.when(j == 0)
    def _():
        h_ref[...] = _rms_norm_rows(x_ref[...], g_ref[...]).astype(BF16)

    def project(w_ref):
        return jnp.dot(h_ref[...], w_ref[...], preferred_element_type=F32)

    @pl.when(j == 0)
    def _():
        acc = project(wqk_ref)
        tm = acc.shape[0]
        cos = cos_ref[...]
        sin = sin_ref[...]
        lane = lax.broadcasted_iota(jnp.int32, (tm, LANES), 1)
        first_half = (lane % DA_QK_DIM) < (DA_QK_DIM // 2)
        for c in range(tn // LANES):
            scale = q_scale if c * LANES < GROUP_W else 1.0
            a = acc[:, c * LANES:(c + 1) * LANES]
            partner = jnp.where(first_half,
                                pltpu.roll(a, LANES - DA_QK_DIM // 2, 1),
                                pltpu.roll(a, DA_QK_DIM // 2, 1))
            qk_ref[:, c * LANES:(c + 1) * LANES] = ((a * cos + partner * sin) * scale).astype(qk_ref.dtype)

    @pl.when(j == 1)
    def _():
        vm_ref[:, :gw] = project(wv_ref).astype(vm_ref.dtype)
        vm_ref[:, gw:] = project(wmq_ref).astype(vm_ref.dtype)

    @pl.when(j == 2)
    def _():
        gate_ref[:, :gw] = project(wag_ref)
        gate_ref[:, gw:] = project(wmg_ref)

    for t in range(PROJ_CHANNEL_TILES):
        @pl.when(j == PROJ_TOKEN_TILES + t)
        def _(t=t):
            hy_ref[0] = lax.dot_general(wt_ref[t * tn:(t + 1) * tn, :], h_ref[...], (((1,), (1,)), ((), ())),
                                        preferred_element_type=F32)


def _input_proj(x, g, w, w_ch_t, rope_tabs, q_scale, batch, seq, tm=1024):
    m, d = x.shape
    tn = PROJ_TN
    gw = GROUP_W
    assert w.shape == (d, 10 * gw) and w_ch_t.shape == (PROJ_CHANNEL_TILES * tn, d) and tn == 2 * gw
    per_batch = seq // tm
    once = pl.Buffered(1)
    clip = lambda j, lo, n: jnp.clip(j - lo, 0, n - 1)
    w_cols = lambda width, blk: pl.BlockSpec((d, width), lambda i, j: (0, blk), pipeline_mode=once)
    return pl.pallas_call(
        functools.partial(_input_proj_kernel, q_scale=q_scale),
        grid=(m // tm, PROJ_TOKEN_TILES + PROJ_CHANNEL_TILES),
        in_specs=[
            pl.BlockSpec((tm, d), lambda i, j: (i, 0)),
            pl.BlockSpec((1, d), lambda i, j: (0, 0)),
            w_cols(2 * gw, 0), w_cols(gw, 2), w_cols(gw, 8), w_cols(gw, 3), w_cols(gw, 9),
            pl.BlockSpec(w_ch_t.shape, lambda i, j: (0, 0), pipeline_mode=once),
            pl.BlockSpec((tm, LANES), lambda i, j: (i % per_batch, 0)),
            pl.BlockSpec((tm, LANES), lambda i, j: (i % per_batch, 0)),
        ],
        out_specs=[
            pl.BlockSpec((tm, tn), lambda i, j: (i, 0)),
            pl.BlockSpec((tm, tn), lambda i, j: (i, 0)),
            pl.BlockSpec((tm, tn), lambda i, j: (i, 0)),
            pl.BlockSpec((1, tn, tm), lambda i, j: (i // per_batch, clip(j, PROJ_TOKEN_TILES, PROJ_CHANNEL_TILES),
                                                    i % per_batch)),
        ],
        out_shape=[
            jax.ShapeDtypeStruct((m, tn), BF16),
            jax.ShapeDtypeStruct((m, tn), BF16),
            jax.ShapeDtypeStruct((m, tn), F32),
            jax.ShapeDtypeStruct((batch, PROJ_CHANNEL_TILES * tn, seq), F32),
        ],
        scratch_shapes=[pltpu.VMEM((tm, d), BF16)],
        compiler_params=_cparams(("parallel", "arbitrary")),
        name="input_proj",
    )(x, g.reshape(1, d), w, w, w, w, w, w_ch_t, rope_tabs[0], rope_tabs[1])


def _diff_attn_kernel(q_ref, k_ref, v_ref, gate_ref, gsub_ref, lq1_ref, lk1_ref, lq2_ref, lk2_ref,
                      o_ref, vt_ref, q2_ref, m_ref, acc_ref, accf_ref, s_a, s_b, cm_a, cm_b, p_a, p_b,
                      *, lam_init, tq, tk, t_chunk):
    seq = k_ref.shape[0]
    dv = DA_V_DIM
    n_tiles = seq // tq
    n_chunks = seq // tk
    assert n_chunks % 2 == 0 and n_chunks >= 4

    row = lax.broadcasted_iota(jnp.int32, (BF16_SUBLANES, t_chunk), 0)
    ones_row = jnp.where(row == 0, 1.0, 0.0).astype(BF16)
    for c in range(seq // t_chunk):
        blk = v_ref[c * t_chunk:(c + 1) * t_chunk, :].astype(F32)
        vt_ref[0:dv, c * t_chunk:(c + 1) * t_chunk] = blk.T.astype(BF16)
        vt_ref[dv:dv + BF16_SUBLANES, c * t_chunk:(c + 1) * t_chunk] = ones_row

    lam = (jnp.exp(jnp.sum(lq1_ref[...] * lk1_ref[...], axis=-1, keepdims=True))
           - jnp.exp(jnp.sum(lq2_ref[...] * lk2_ref[...], axis=-1, keepdims=True)) + lam_init)

    def tile_rows(i):
        return pl.ds(pl.multiple_of(i * tq, tq), tq)

    def load_queries(i):
        q = q_ref[tile_rows(i), :]
        lane = lax.broadcasted_iota(jnp.int32, q.shape, 1)
        zero = jnp.zeros_like(q)
        q2_ref[...] = jnp.concatenate([jnp.where(lane < DA_QK_DIM, q, zero),
                                       jnp.where(lane >= DA_QK_DIM, q, zero)], axis=0)

    def finish_tile(i):
        for g in range(tq // LANES):
            rows = pl.ds(pl.multiple_of(i * tq + g * LANES, LANES), LANES)
            c0 = slice(g * LANES, (g + 1) * LANES)
            c1 = slice(tq + g * LANES, tq + (g + 1) * LANES)
            ot = (accf_ref[0:dv, c0] / accf_ref[dv:dv + 1, c0]
                  - lam * (accf_ref[0:dv, c1] / accf_ref[dv:dv + 1, c1]))
            y = _rms_norm_rows(ot.T, gsub_ref[...]) * (1.0 - lam_init)
            o_ref[rows, :] = (y * _silu(gate_ref[rows, :])).astype(o_ref.dtype)

    def scores(c, s_ref, cm_ref):
        kc = k_ref[pl.ds(pl.multiple_of(c * tk, tk), tk), :]
        s = lax.dot_general(kc, q2_ref[...], (((1,), (1,)), ((), ())), preferred_element_type=F32)
        s_ref[...] = s
        cm_ref[...] = jnp.max(s, axis=0, keepdims=True)

    def weighted_values(c, p_ref):
        vtc = vt_ref[:, pl.ds(pl.multiple_of(c * tk, tk), tk)]
        return jnp.dot(vtc, p_ref[...], preferred_element_type=F32)

    def softmax_chunk(s_ref, cm_ref, p_ref):
        m_prev = m_ref[...]
        m_new = jnp.maximum(m_prev, cm_ref[...])
        p_ref[...] = jnp.exp2(s_ref[...] - m_new).astype(BF16)
        m_ref[...] = m_new
        return jnp.exp2(m_prev - m_new)

    def step(c, s_cur, cm_cur, s_next, cm_next, p_cur, p_prev):
        scores(c + 1, s_next, cm_next)
        pending = weighted_values(c - 1, p_prev)
        alpha = softmax_chunk(s_cur, cm_cur, p_cur)
        acc_ref[...] = (acc_ref[...] + pending) * alpha

    acc_ref[...] = jnp.zeros(acc_ref.shape, F32)
    p_b[...] = jnp.zeros(p_b.shape, BF16)
    load_queries(0)
    scores(0, s_a, cm_a)

    even = (s_a, cm_a, s_b, cm_b, p_a, p_b)
    odd = (s_b, cm_b, s_a, cm_a, p_b, p_a)

    def tile(i, carry):
        scores(1, s_b, cm_b)
        accf_ref[...] = acc_ref[...] + weighted_values(n_chunks - 1, p_b)
        m_ref[...] = jnp.full(m_ref.shape, -1e30, F32)
        softmax_chunk(s_a, cm_a, p_a)
        acc_ref[...] = jnp.zeros(acc_ref.shape, F32)
        step(1, *odd)

        @pl.when(i > 0)
        def _():
            finish_tile(i - 1)

        def pair(j, carry):
            c = 2 + 2 * j
            step(c, *even)
            step(c + 1, *odd)
            return carry
        lax.fori_loop(0, (n_chunks - 4) // 2, pair, 0)

        step(n_chunks - 2, *even)
        load_queries(jnp.minimum(i + 1, n_tiles - 1))
        scores(0, s_a, cm_a)
        pending = weighted_values(n_chunks - 2, p_a)
        alpha = softmax_chunk(s_b, cm_b, p_b)
        acc_ref[...] = (acc_ref[...] + pending) * alpha
        return carry
    lax.fori_loop(0, n_tiles, tile, 0)

    accf_ref[...] = acc_ref[...] + weighted_values(n_chunks - 1, p_b)
    finish_tile(n_tiles - 1)


def _diff_attention(qk, vm, gates, g_sub, lq1, lk1, lq2, lk2, lam_init, batch, seq, tq=2048, tk=512):
    head_blocks = GROUP_W // LANES
    vec = lambda a: a.reshape(1, -1).astype(F32)
    small = lambda n: pl.BlockSpec((1, n), lambda b, h: (0, 0))
    once = pl.Buffered(1)
    return pl.pallas_call(
        functools.partial(_diff_attn_kernel, lam_init=lam_init, tq=tq, tk=tk, t_chunk=512),
        grid=(batch, DA_HEADS),
        in_specs=[
            pl.BlockSpec((seq, LANES), lambda b, h: (b, h), pipeline_mode=once),
            pl.BlockSpec((seq, LANES), lambda b, h: (b, head_blocks + h)),
            pl.BlockSpec((seq, LANES), lambda b, h: (b, h)),
            pl.BlockSpec((seq, LANES), lambda b, h: (b, h), pipeline_mode=once),
            small(DA_V_DIM), small(DA_QK_DIM), small(DA_QK_DIM), small(DA_QK_DIM), small(DA_QK_DIM),
        ],
        out_specs=pl.BlockSpec((seq, LANES), lambda b, h: (b, h)),
        out_shape=jax.ShapeDtypeStruct((batch * seq, GROUP_W), BF16),
        scratch_shapes=[
            pltpu.VMEM((DA_V_DIM + BF16_SUBLANES, seq), BF16),
            pltpu.VMEM((2 * tq, LANES), BF16),
            pltpu.VMEM((1, 2 * tq), F32),
            pltpu.VMEM((DA_V_DIM + BF16_SUBLANES, 2 * tq), F32),
            pltpu.VMEM((DA_V_DIM + BF16_SUBLANES, 2 * tq), F32),
            pltpu.VMEM((tk, 2 * tq), F32), pltpu.VMEM((tk, 2 * tq), F32),
            pltpu.VMEM((1, 2 * tq), F32), pltpu.VMEM((1, 2 * tq), F32),
            pltpu.VMEM((tk, 2 * tq), BF16), pltpu.VMEM((tk, 2 * tq), BF16),
        ],
        compiler_params=pltpu.CompilerParams(dimension_semantics=("parallel", "parallel"),
                                             vmem_limit_bytes=ATTN_VMEM_LIMIT),
        name="diff_attention",
    )(qk, qk, vm, gates, vec(g_sub), vec(lq1), vec(lk1), vec(lq2), vec(lk2))


def _filter_kernel(z_ref, w1_ref, b1_ref, w2_ref, b2_ref, w3_ref, b3_ref, w4_ref, fr_ref, delta_ref, o_ref):
    hp = lax.Precision.HIGHEST
    fr = fr_ref[...]

    def dot_split(a, b):
        a_hi = a.astype(BF16)
        a_lo = (a - a_hi.astype(F32)).astype(BF16)
        b_hi = b.astype(BF16)
        b_lo = (b - b_hi.astype(F32)).astype(BF16)
        return (jnp.dot(a_hi, b_hi, preferred_element_type=F32)
                + (jnp.dot(a_hi, b_lo, preferred_element_type=F32)
                   + jnp.dot(a_lo, b_hi, preferred_element_type=F32)))

    z = z_ref[...]
    h = jnp.sin(fr * (jnp.dot(w1_ref[...], z, precision=hp, preferred_element_type=F32) + b1_ref[...]))
    h = jnp.sin(fr * (jnp.dot(w2_ref[...], h, precision=hp, preferred_element_type=F32) + b2_ref[...]))
    h = jnp.sin(fr * (jnp.dot(w3_ref[...], h, precision=hp, preferred_element_type=F32) + b3_ref[...]))
    decay = jnp.exp(-delta_ref[...] * z[0:1, :])
    o_ref[...] = (dot_split(w4_ref[...], h) * decay).astype(o_ref.dtype)


def _hyena_filters(zt, deltas, w1, b1, w2, b2, w3, b3, w4, freq, tl=512):
    seq = zt.shape[1]
    k = HF_ORDER
    n_out = w4.shape[1]
    w1t = jnp.zeros((k, k), F32).at[:, :HF_EMB].set(w1.astype(F32).T)
    colv = lambda a: a.reshape(-1, 1).astype(F32)
    full = lambda shape: pl.BlockSpec(shape, lambda i: (0, 0))
    return pl.pallas_call(
        _filter_kernel,
        grid=(seq // tl,),
        in_specs=[
            pl.BlockSpec((k, tl), lambda i: (0, i)),
            full((k, k)), full((k, 1)), full((k, k)), full((k, 1)), full((k, k)), full((k, 1)),
            full((n_out, k)), full((k, 1)), full((n_out, 1)),
        ],
        out_specs=pl.BlockSpec((n_out, tl), lambda i: (0, i)),
        out_shape=jax.ShapeDtypeStruct((n_out, seq), BF16),
        compiler_params=_cparams(("parallel",)),
        name="hyena_filters",
    )(zt, w1t, colv(b1), w2.astype(F32).T, colv(b2), w3.astype(F32).T, colv(b3), w4.astype(F32).T,
      colv(freq), deltas)


def _dft_tables():
    k1 = np.arange(FFT_R, dtype=np.int64)[:, None]
    n1 = np.arange(FFT_HALF, dtype=np.int64)[None, :]
    a1 = -2.0 * np.pi * ((k1 * n1) % FFT_R) / FFT_R
    f1 = np.concatenate([np.cos(a1), np.sin(a1)], axis=0)
    f1_inv = np.concatenate([np.cos(a1).T, np.sin(a1).T], axis=1)
    n2 = np.arange(FFT_R, dtype=np.int64)[None, :]
    at = -2.0 * np.pi * ((k1 * n2) % FFT_N) / FFT_N
    k2 = np.arange(FFT_R, dtype=np.int64)[None, :]
    a2 = -2.0 * np.pi * ((n2.T * k2) % FFT_R) / FFT_R
    fr, fi = np.cos(a2), np.sin(a2)
    c2 = np.block([[fr, fi], [-fi, fr]])
    c2_inv = np.block([[fr, -fi], [fi, fr]])
    f1r, f1i = np.cos(a1), np.sin(a1)
    f1c = np.block([[f1r, -f1i], [f1i, f1r]])
    f1c_inv = np.block([[f1r.T, f1i.T], [-f1i.T, f1r.T]])
    as_bf16 = lambda a: jnp.asarray(a, F32).astype(BF16)
    return dict(f1=as_bf16(f1), f1_inv=as_bf16(f1_inv), c2=as_bf16(c2), c2_inv=as_bf16(c2_inv),
                f1c=as_bf16(f1c), f1c_inv=as_bf16(f1c_inv),
                tr=jnp.asarray(np.cos(at), F32), ti=jnp.asarray(np.sin(at), F32))


def _dft_many(xs, f1_ref, c2_ref, tr, ti):
    r = FFT_R
    xx = jnp.concatenate([x.astype(BF16) for x in xs], axis=1)
    a = jnp.dot(f1_ref[...], xx, preferred_element_type=F32)
    rows = []
    for s in range(len(xs)):
        ar, ai = a[:r, s * r:(s + 1) * r], a[r:, s * r:(s + 1) * r]
        rows.append(jnp.concatenate([ar * tr - ai * ti, ar * ti + ai * tr], axis=1).astype(BF16))
    return jnp.dot(jnp.concatenate(rows, axis=0), c2_ref[...], preferred_element_type=F32)


def _idft_many(ys, f1i_ref, c2i_ref, tr, ti):
    r = FFT_R
    y = jnp.concatenate([v.astype(BF16) for v in ys], axis=0)
    b = jnp.dot(y, c2i_ref[...], preferred_element_type=F32)
    outs_r, outs_i = [], []
    for s in range(len(ys)):
        br, bi = b[s * r:(s + 1) * r, :r], b[s * r:(s + 1) * r, r:]
        outs_r.append((br * tr + bi * ti).astype(BF16))
        outs_i.append((bi * tr - br * ti).astype(BF16))
    rhs = jnp.concatenate([jnp.concatenate(outs_r, axis=1), jnp.concatenate(outs_i, axis=1)], axis=0)
    return jnp.dot(f1i_ref[...], rhs, preferred_element_type=F32)


def _delay_one(u):
    lane = lax.broadcasted_iota(jnp.int32, u.shape, 1)
    row = lax.broadcasted_iota(jnp.int32, u.shape, 0)
    prev = pltpu.roll(u, 1, 1)
    return jnp.where(lane == 0, jnp.where(row == 0, 0.0, pltpu.roll(prev, 1, 0)), prev)


def _filter_spectrum_kernel(hf_ref, hb_ref, f1_ref, c2_ref, tr_ref, ti_ref, o_ref):
    r = FFT_R
    cb = hf_ref.shape[2]
    xs = []
    for ci in range(cb):
        xs.append(hf_ref[0, 0, ci])
        xs.append(_delay_one(hb_ref[0, 0, ci].astype(F32)))
    x = _dft_many(xs, f1_ref, c2_ref, tr_ref[...], ti_ref[...])
    for ci in range(cb):
        xf, xb = x[2 * ci * r:(2 * ci + 1) * r], x[(2 * ci + 1) * r:(2 * ci + 2) * r]
        o_ref[0, ci] = jnp.concatenate([xf[:, :r] + xb[:, :r], xf[:, r:] - xb[:, r:]],
                                       axis=1).astype(o_ref.dtype)


def _filter_spectrum(filt5, tabs):
    n_ord, _, n_ch, rows, _ = filt5.shape
    cb = HY_CBLK
    const = lambda shape: pl.BlockSpec(shape, lambda o, c: (0, 0))
    return pl.pallas_call(
        _filter_spectrum_kernel,
        grid=(n_ord, n_ch // cb),
        in_specs=[
            pl.BlockSpec((1, 1, cb, rows, LANES), lambda o, c: (o, 0, c, 0, 0)),
            pl.BlockSpec((1, 1, cb, rows, LANES), lambda o, c: (o, 1, c, 0, 0)),
            const((2 * FFT_R, FFT_HALF)), const((2 * FFT_R, 2 * FFT_R)),
            const((FFT_R, FFT_R)), const((FFT_R, FFT_R)),
        ],
        out_specs=pl.BlockSpec((1, cb, FFT_R, 2 * FFT_R), lambda o, c: (o, c, 0, 0)),
        out_shape=jax.ShapeDtypeStruct((n_ord, n_ch, FFT_R, 2 * FFT_R), BF16),
        compiler_params=_cparams(("parallel", "parallel")),
        name="filter_spectrum",
    )(filt5, filt5, tabs["f1"], tabs["c2"], tabs["tr"], tabs["ti"])


def _short_conv_mat(u, w0, w1, w2, bias):
    rows, lanes = u.shape
    lane = lax.broadcasted_iota(jnp.int32, u.shape, 1)
    row = lax.broadcasted_iota(jnp.int32, u.shape, 0)
    prev = _delay_one(u)
    nxt = pltpu.roll(u, lanes - 1, 1)
    nxt = jnp.where(lane == lanes - 1, jnp.where(row == rows - 1, 0.0, pltpu.roll(nxt, rows - 1, 0)), nxt)
    return w0 * prev + w1 * u + w2 * nxt + bias


def _hyena_conv_kernel(cw_ref, d_ref, u_ref, gate_ref, *refs, conv_in, gated_out, u_ch0, gate_ch0, order_ch0):
    if gated_out:
        g2_ref, k_ref, f1_ref, f1i_ref, c2_ref, c2i_ref, tr_ref, ti_ref, o_ref = refs
    else:
        k_ref, f1_ref, f1i_ref, c2_ref, c2i_ref, tr_ref, ti_ref, o_ref = refs
        g2_ref = None
    r = FFT_R
    tr, ti = tr_ref[...], ti_ref[...]
    batch, cb = u_ref.shape[0], u_ref.shape[1]
    c0 = pl.program_id(0) * cb
    signals = [(ci, b) for ci in range(cb) for b in range(batch)]

    def conv_taps(ref, b, ci, ch):
        return _short_conv_mat(ref[b, ci], cw_ref[0, ch], cw_ref[1, ch], cw_ref[2, ch], cw_ref[3, ch])

    def long_conv_input(ci, b):
        return conv_taps(u_ref, b, ci, u_ch0 + c0 + ci) if conv_in else u_ref[b, ci]

    assert batch == 2
    half = FFT_HALF
    us = [long_conv_input(ci, b) for ci, b in signals]
    packed = [jnp.concatenate([us[2 * ci], us[2 * ci + 1]], axis=0) for ci in range(cb)]
    x = _dft_many(packed, f1_ref, c2_ref, tr, ti)
    ys = []
    for ci in range(cb):
        kk = k_ref[0, ci].astype(F32)
        kr, ki = kk[:, :r], kk[:, r:]
        xr, xi = x[ci * r:(ci + 1) * r, :r], x[ci * r:(ci + 1) * r, r:]
        ys.append(jnp.concatenate([xr * kr - xi * ki, xr * ki + xi * kr], axis=1))
    y = _idft_many(ys, f1i_ref, c2i_ref, tr, ti)
    for s, (ci, b) in enumerate(signals):
        gate = conv_taps(gate_ref, b, ci, gate_ch0 + c0 + ci)
        skip = d_ref[0, order_ch0 + c0 + ci] * us[s]
        res = gate * (y[b * half:(b + 1) * half, ci * r:(ci + 1) * r] * (1.0 / FFT_N) + skip)
        if gated_out:
            res = res * _silu(g2_ref[b, ci])
        o_ref[b, ci] = res.astype(o_ref.dtype)


def _hyena_conv(conv_wb, d_skip, u_arr, u_ch0, gate_arr, gate_ch0, g2, spec, order, tabs, out_dtype, conv_in):
    batch = u_arr.shape[0]
    assert batch == 2 and u_arr.shape[2] == FFT_HALF
    cb = HY_CBLK
    smem = pl.BlockSpec(memory_space=pltpu.SMEM)
    chan = lambda ch0: pl.BlockSpec((batch, cb, FFT_HALF, LANES), lambda c: (0, ch0 // cb + c, 0, 0))
    const = lambda shape: pl.BlockSpec(shape, lambda c: (0, 0))
    in_specs = [smem, smem, chan(u_ch0), chan(gate_ch0)]
    args = [conv_wb, d_skip, u_arr, gate_arr]
    if g2 is not None:
        in_specs.append(chan(g2[1]))
        args.append(g2[0])
    in_specs += [
        pl.BlockSpec((1, cb, FFT_R, 2 * FFT_R), lambda c: (order, c, 0, 0)),
        const((2 * FFT_R, FFT_R)), const((FFT_R, 2 * FFT_R)),
        const((2 * FFT_R, 2 * FFT_R)), const((2 * FFT_R, 2 * FFT_R)),
        const((FFT_R, FFT_R)), const((FFT_R, FFT_R)),
    ]
    args += [spec, tabs["f1c"], tabs["f1c_inv"], tabs["c2"], tabs["c2_inv"], tabs["tr"], tabs["ti"]]
    return pl.pallas_call(
        functools.partial(_hyena_conv_kernel, conv_in=conv_in, gated_out=g2 is not None,
                          u_ch0=u_ch0, gate_ch0=gate_ch0, order_ch0=order * HY_CH),
        grid=(HY_CH // cb,),
        in_specs=in_specs,
        out_specs=pl.BlockSpec((batch, cb, FFT_HALF, LANES), lambda c: (0, c, 0, 0)),
        out_shape=jax.ShapeDtypeStruct((batch, HY_CH, FFT_HALF, LANES), out_dtype),
        compiler_params=_cparams(("parallel",)),
        name="hyena_conv_gated" if g2 is not None else "hyena_conv",
    )(*args)


def _out_kernel(ya_ref, ybt_ref, mq_ref, mg_ref, mk_ref, mv_ref, w_ref, x_ref, gf_ref, o_ref, *, final):
    gw = GROUP_W
    acc = x_ref[...]
    acc += jnp.dot(ya_ref[...], w_ref[0:gw, :], preferred_element_type=F32)
    acc += lax.dot_general(ybt_ref[0], w_ref[gw:2 * gw, :], (((0,), (0,)), ((), ())),
                           preferred_element_type=F32)
    hd = MEM_HEAD_DIM
    heads = []
    for h in range(MEM_HEADS):
        sl = slice(h * hd, (h + 1) * hd)
        s = lax.dot_general(mq_ref[:, sl], mk_ref[:, sl], (((1,), (1,)), ((), ())),
                            preferred_element_type=F32) * (hd ** -0.5)
        e = jnp.exp(s - jnp.max(s, axis=-1, keepdims=True))
        p = e / jnp.sum(e, axis=-1, keepdims=True)
        oh = jnp.dot(p.astype(BF16), mv_ref[:, sl], preferred_element_type=F32)
        heads.append((oh * _silu(mg_ref[:, sl])).astype(BF16))
    acc += jnp.dot(jnp.concatenate(heads, axis=1), w_ref[2 * gw:3 * gw, :], preferred_element_type=F32)
    if final:
        acc = _rms_norm_rows(acc, gf_ref[...])
    o_ref[...] = acc


def _out_proj(ya, ybt, vm, feats, mg_blk, mkv, w_out, x, g_final, final, seq, mem_len, tm=1024):
    m, d = x.shape
    gw = GROUP_W
    per_batch = seq // tm
    return pl.pallas_call(
        functools.partial(_out_kernel, final=final),
        grid=(m // tm,),
        in_specs=[
            pl.BlockSpec((tm, gw), lambda i: (i, 0)),
            pl.BlockSpec((1, gw, tm), lambda i: (i // per_batch, 0, i % per_batch)),
            pl.BlockSpec((tm, gw), lambda i: (i, 1)),
            pl.BlockSpec((tm, gw), lambda i: (i, mg_blk)),
            pl.BlockSpec((mem_len, gw), lambda i: (i // per_batch, 0)),
            pl.BlockSpec((mem_len, gw), lambda i: (i // per_batch, 1)),
            pl.BlockSpec((3 * gw, d), lambda i: (0, 0)),
            pl.BlockSpec((tm, d), lambda i: (i, 0)),
            pl.BlockSpec((1, d), lambda i: (0, 0)),
        ],
        out_specs=pl.BlockSpec((tm, d), lambda i: (i, 0)),
        out_shape=jax.ShapeDtypeStruct((m, d), F32),
        compiler_params=_cparams(("parallel",)),
        name="out_proj_final" if final else "out_proj",
    )(ya, ybt, vm, feats, mkv, mkv, w_out, x, g_final.reshape(1, d).astype(F32))


def _rope_tables(seq):
    pos = np.arange(seq, dtype=np.float64)
    inv_freq = ROPE_THETA ** (-np.arange(0, DA_QK_DIM, 2, dtype=np.float64) / DA_QK_DIM)
    ang = pos[:, None] * inv_freq[None, :]
    cos, sin = np.cos(ang), np.sin(ang)
    reps = LANES // DA_QK_DIM
    return (jnp.asarray(np.tile(np.concatenate([cos, cos], axis=-1), (1, reps)), F32),
            jnp.asarray(np.tile(np.concatenate([-sin, sin], axis=-1), (1, reps)), F32))


def _filter_features(seq):
    t = np.linspace(0.0, 1.0, seq)[:, None]
    w = 2.0 * math.pi * np.arange(seq, dtype=np.float64)[:, None] / seq
    f = np.linspace(1e-4, HF_BANDS - 1, HF_BANDS)[None, :]
    z = np.concatenate([t, np.cos(f * w), -np.sin(f * w)], axis=-1)
    z = np.pad(z, ((0, 0), (0, HF_ORDER - HF_EMB)))
    max_decay = math.log(HF_TARGET) / HF_FAST
    min_decay = math.log(HF_TARGET) / HF_SLOW
    deltas = np.abs(np.linspace(min_decay, max_decay, HY_CH))
    per_order = np.concatenate([deltas, deltas[::-1]])
    as_f32 = lambda a: jnp.asarray(np.ascontiguousarray(a), F32)
    return as_f32(z.T), as_f32(np.tile(per_order, 2)[:, None])


def kernel(x, mem, g_norm, w_in, da_lam_q1, da_lam_k1, da_lam_q2, da_lam_k2, da_subln_g, hy_conv_w, hy_conv_b, hf_w1, hf_b1, hf_w2, hf_b2, hf_w3, hf_b3, hf_w4, hf_freq, hy_skip, g_mem, w_mem_kv, w_out, g_final):
    batch, seq, d = x.shape
    assert seq == SEQ_LEN
    mem_len = mem.shape[1]
    depth = w_in.shape[0]
    gw = GROUP_W
    xf = x.reshape(batch * seq, d).astype(F32)
    memf = mem.reshape(batch * mem_len, d).astype(F32)
    rope_tabs = _rope_tables(seq)
    zt, deltas = _filter_features(seq)
    tabs = _dft_tables()

    for l in range(depth):
        w = w_in[l].astype(BF16)
        w_hy_t = w[:, 4 * gw:8 * gw].T
        qk, vm, gates, hy = _input_proj(xf, g_norm[l], w, w_hy_t, rope_tabs,
                                        DA_QK_DIM ** -0.5 * math.log2(math.e),
                                        batch, seq)
        hy = hy.reshape(batch, 4 * HY_CH, FFT_HALF, LANES)
        mkv = _norm_matmul(memf, g_mem[l], w_mem_kv[l].astype(BF16), BF16, "proj_mem_kv")

        lam_init = 0.8 - 0.6 * math.exp(-0.3 * l)
        ya = _diff_attention(qk, vm, gates, da_subln_g[l], da_lam_q1[l], da_lam_k1[l],
                             da_lam_q2[l], da_lam_k2[l], lam_init, batch, seq)

        filt = _hyena_filters(zt, deltas, hf_w1[l], hf_b1[l], hf_w2[l], hf_b2[l],
                              hf_w3[l], hf_b3[l], hf_w4[l], hf_freq[l])
        spec = _filter_spectrum(filt.reshape(2, 2, HY_CH, FFT_HALF, LANES), tabs)
        conv_wb = jnp.concatenate([hy_conv_w[l], hy_conv_b[l][None, :]], axis=0).astype(F32)
        d_skip = hy_skip[l].reshape(1, -1).astype(F32)
        z1 = _hyena_conv(conv_wb, d_skip, hy, 0, hy, HY_CH, None, spec, 0, tabs, F32, conv_in=True)
        ybt = _hyena_conv(conv_wb, d_skip, z1, 0, hy, 2 * HY_CH, (hy, 3 * HY_CH), spec, 1, tabs, BF16,
                          conv_in=False)
        ybt = ybt.reshape(batch, HY_CH, seq)

        xf = _out_proj(ya, ybt, vm, gates, 1, mkv, w_out[l].astype(BF16), xf, g_final,
                       l == depth - 1, seq, mem_len)
    return xf.reshape(batch, seq, d)
```

```python
import functools
import math

import jax
import jax.numpy as jnp
import numpy as np
from jax import lax
from jax.experimental import pallas as pl
from jax.experimental.pallas import tpu as pltpu

F32 = jnp.float32
BF16 = jnp.bfloat16

GROUP_W = 512
DA_HEADS = 4
DA_QK_DIM = 64
DA_V_DIM = 128
HY_CH = 512
HF_EMB = 33
HF_BANDS = 16
HF_ORDER = 64
HF_TARGET = 1e-2
HF_FAST = 0.3
HF_SLOW = 1.5
MEM_HEADS = 4
MEM_HEAD_DIM = 128
ROPE_THETA = 10000.0
EPS = 1e-6

LANES = 128
BF16_SUBLANES = 16
VMEM_LIMIT = 56 * 1024 * 1024
ATTN_VMEM_LIMIT = 60 * 1024 * 1024

FFT_R = 128
FFT_N = FFT_R * FFT_R
SEQ_LEN = FFT_N // 2
FFT_HALF = FFT_R // 2
HY_CBLK = 16


def _cparams(sem):
    return pltpu.CompilerParams(dimension_semantics=sem, vmem_limit_bytes=VMEM_LIMIT)


def _silu(g):
    return g * (1.0 / (1.0 + jnp.exp(-g)))


def _rms_norm_rows(x, g):
    ms = jnp.mean(x * x, axis=-1, keepdims=True)
    return x * lax.rsqrt(ms + EPS) * g


def _norm_matmul_kernel(x_ref, g_ref, w_ref, *refs, rope, q_scale):
    if rope:
        cos_ref, sin_ref, o_ref, h_ref = refs
    else:
        o_ref, h_ref = refs
    j = pl.program_id(1)

    @pl.when(j == 0)
    def _():
        h_ref[...] = _rms_norm_rows(x_ref[...], g_ref[...]).astype(BF16)

    acc = jnp.dot(h_ref[...], w_ref[...], preferred_element_type=F32)
    if rope:
        tm, tn = acc.shape
        scale = jnp.where(j == 0, q_scale, 1.0).astype(F32)
        cos = cos_ref[...]
        sin = sin_ref[...]
        lane = lax.broadcasted_iota(jnp.int32, (tm, LANES), 1)
        first_half = (lane % DA_QK_DIM) < (DA_QK_DIM // 2)
        for c in range(tn // LANES):
            a = acc[:, c * LANES:(c + 1) * LANES]
            partner = jnp.where(first_half,
                                pltpu.roll(a, LANES - DA_QK_DIM // 2, 1),
                                pltpu.roll(a, DA_QK_DIM // 2, 1))
            o_ref[:, c * LANES:(c + 1) * LANES] = ((a * cos + partner * sin) * scale).astype(o_ref.dtype)
    else:
        o_ref[...] = acc.astype(o_ref.dtype)


def _norm_matmul(x, g, w, out_dtype, name, rope_tabs=None, q_scale=1.0, tm=1024, tn=512):
    m, d = x.shape
    n = w.shape[1]
    tm = min(tm, m)
    assert m % tm == 0 and n % tn == 0
    in_specs = [
        pl.BlockSpec((tm, d), lambda i, j: (i, 0)),
        pl.BlockSpec((1, d), lambda i, j: (0, 0)),
        pl.BlockSpec((d, tn), lambda i, j: (0, j)),
    ]
    args = [x, g.reshape(1, d), w]
    if rope_tabs is not None:
        n_pos = rope_tabs[0].shape[0] // tm
        for t in rope_tabs:
            in_specs.append(pl.BlockSpec((tm, LANES), lambda i, j: (i % n_pos, 0)))
            args.append(t)
    return pl.pallas_call(
        functools.partial(_norm_matmul_kernel, rope=rope_tabs is not None, q_scale=q_scale),
        grid=(m // tm, n // tn),
        in_specs=in_specs,
        out_specs=pl.BlockSpec((tm, tn), lambda i, j: (i, j)),
        out_shape=jax.ShapeDtypeStruct((m, n), out_dtype),
        scratch_shapes=[pltpu.VMEM((tm, d), BF16)],
        compiler_params=_cparams(("parallel", "arbitrary")),
        name=name,
    )(*args)


PROJ_TN = 1024
PROJ_TOKEN_TILES = 3
PROJ_CHANNEL_TILES = 2


def _input_proj_kernel(x_ref, g_ref, wqk_ref, wv_ref, wmq_ref, wag_ref, wmg_ref, wt_ref, cos_ref, sin_ref,
                       qk_ref, vm_ref, gate_ref, hy_ref, h_ref, *, q_scale):
    j = pl.program_id(1)
    tn = PROJ_TN
    gw = GROUP_W

    @pl.when(j == 0)
    def _():
        h_ref[...] = _rms_norm_rows(x_ref[...], g_ref[...]).astype(BF16)

    def project(w_ref):
        return jnp.dot(h_ref[...], w_ref[...], preferred_element_type=F32)

    @pl.when(j == 0)
    def _():
        acc = project(wqk_ref)
        tm = acc.shape[0]
        cos = cos_ref[...]
        sin = sin_ref[...]
        lane = lax.broadcasted_iota(jnp.int32, (tm, LANES), 1)
        first_half = (lane % DA_QK_DIM) < (DA_QK_DIM // 2)
        for c in range(tn // LANES):
            scale = q_scale if c * LANES < GROUP_W else 1.0
            a = acc[:, c * LANES:(c + 1) * LANES]
            partner = jnp.where(first_half,
                                pltpu.roll(a, LANES - DA_QK_DIM // 2, 1),
                                pltpu.roll(a, DA_QK_DIM // 2, 1))
            qk_ref[:, c * LANES:(c + 1) * LANES] = ((a * cos + partner * sin) * scale).astype(qk_ref.dtype)

    @pl.when(j == 1)
    def _():
        vm_ref[:, :gw] = project(wv_ref).astype(vm_ref.dtype)
        vm_ref[:, gw:] = project(wmq_ref).astype(vm_ref.dtype)

    @pl.when(j == 2)
    def _():
        gate_ref[:, :gw] = project(wag_ref)
        gate_ref[:, gw:] = project(wmg_ref)

    for t in range(PROJ_CHANNEL_TILES):
        @pl.when(j == PROJ_TOKEN_TILES + t)
        def _(t=t):
            hy_ref[0] = lax.dot_general(wt_ref[t * tn:(t + 1) * tn, :], h_ref[...], (((1,), (1,)), ((), ())),
                                        preferred_element_type=F32)


def _input_proj(x, g, w, w_ch_t, rope_tabs, q_scale, batch, seq, tm=1024):
    m, d = x.shape
    tn = PROJ_TN
    gw = GROUP_W
    assert w.shape == (d, 10 * gw) and w_ch_t.shape == (PROJ_CHANNEL_TILES * tn, d) and tn == 2 * gw
    per_batch = seq // tm
    once = pl.Buffered(1)
    clip = lambda j, lo, n: jnp.clip(j - lo, 0, n - 1)
    w_cols = lambda width, blk: pl.BlockSpec((d, width), lambda i, j: (0, blk), pipeline_mode=once)
    return pl.pallas_call(
        functools.partial(_input_proj_kernel, q_scale=q_scale),
        grid=(m // tm, PROJ_TOKEN_TILES + PROJ_CHANNEL_TILES),
        in_specs=[
            pl.BlockSpec((tm, d), lambda i, j: (i, 0)),
            pl.BlockSpec((1, d), lambda i, j: (0, 0)),
            w_cols(2 * gw, 0), w_cols(gw, 2), w_cols(gw, 8), w_cols(gw, 3), w_cols(gw, 9),
            pl.BlockSpec(w_ch_t.shape, lambda i, j: (0, 0), pipeline_mode=once),
            pl.BlockSpec((tm, LANES), lambda i, j: (i % per_batch, 0)),
            pl.BlockSpec((tm, LANES), lambda i, j: (i % per_batch, 0)),
        ],
        out_specs=[
            pl.BlockSpec((tm, tn), lambda i, j: (i, 0)),
            pl.BlockSpec((tm, tn), lambda i, j: (i, 0)),
            pl.BlockSpec((tm, tn), lambda i, j: (i, 0)),
            pl.BlockSpec((1, tn, tm), lambda i, j: (i // per_batch, clip(j, PROJ_TOKEN_TILES, PROJ_CHANNEL_TILES),
                                                    i % per_batch)),
        ],
        out_shape=[
            jax.ShapeDtypeStruct((m, tn), BF16),
            jax.ShapeDtypeStruct((m, tn), BF16),
            jax.ShapeDtypeStruct((m, tn), F32),
            jax.ShapeDtypeStruct((batch, PROJ_CHANNEL_TILES * tn, seq), F32),
        ],
        scratch_shapes=[pltpu.VMEM((tm, d), BF16)],
        compiler_params=_cparams(("parallel", "arbitrary")),
        name="input_proj",
    )(x, g.reshape(1, d), w, w, w, w, w, w_ch_t, rope_tabs[0], rope_tabs[1])


def _diff_attn_kernel(q_ref, k_ref, v_ref, gate_ref, gsub_ref, lq1_ref, lk1_ref, lq2_ref, lk2_ref,
                      o_ref, vt_ref, q2_ref, m_ref, acc_ref, accf_ref, s_a, s_b, cm_a, cm_b, p_a, p_b,
                      *, lam_init, tq, tk, t_chunk):
    seq = k_ref.shape[0]
    dv = DA_V_DIM
    n_tiles = seq // tq
    n_chunks = seq // tk
    assert n_chunks % 2 == 0 and n_chunks >= 4

    row = lax.broadcasted_iota(jnp.int32, (BF16_SUBLANES, t_chunk), 0)
    ones_row = jnp.where(row == 0, 1.0, 0.0).astype(BF16)
    for c in range(seq // t_chunk):
        blk = v_ref[c * t_chunk:(c + 1) * t_chunk, :].astype(F32)
        vt_ref[0:dv, c * t_chunk:(c + 1) * t_chunk] = blk.T.astype(BF16)
        vt_ref[dv:dv + BF16_SUBLANES, c * t_chunk:(c + 1) * t_chunk] = ones_row

    lam = (jnp.exp(jnp.sum(lq1_ref[...] * lk1_ref[...], axis=-1, keepdims=True))
           - jnp.exp(jnp.sum(lq2_ref[...] * lk2_ref[...], axis=-1, keepdims=True)) + lam_init)

    def tile_rows(i):
        return pl.ds(pl.multiple_of(i * tq, tq), tq)

    def load_queries(i):
        q = q_ref[tile_rows(i), :]
        lane = lax.broadcasted_iota(jnp.int32, q.shape, 1)
        zero = jnp.zeros_like(q)
        q2_ref[...] = jnp.concatenate([jnp.where(lane < DA_QK_DIM, q, zero),
                                       jnp.where(lane >= DA_QK_DIM, q, zero)], axis=0)

    def finish_tile(i):
        for g in range(tq // LANES):
            rows = pl.ds(pl.multiple_of(i * tq + g * LANES, LANES), LANES)
            c0 = slice(g * LANES, (g + 1) * LANES)
            c1 = slice(tq + g * LANES, tq + (g + 1) * LANES)
            ot = (accf_ref[0:dv, c0] / accf_ref[dv:dv + 1, c0]
                  - lam * (accf_ref[0:dv, c1] / accf_ref[dv:dv + 1, c1]))
            y = _rms_norm_rows(ot.T, gsub_ref[...]) * (1.0 - lam_init)
            o_ref[rows, :] = (y * _silu(gate_ref[rows, :])).astype(o_ref.dtype)

    def scores(c, s_ref, cm_ref):
        kc = k_ref[pl.ds(pl.multiple_of(c * tk, tk), tk), :]
        s = lax.dot_general(kc, q2_ref[...], (((1,), (1,)), ((), ())), preferred_element_type=F32)
        s_ref[...] = s
        cm_ref[...] = jnp.max(s, axis=0, keepdims=True)

    def weighted_values(c, p_ref):
        vtc = vt_ref[:, pl.ds(pl.multiple_of(c * tk, tk), tk)]
        return jnp.dot(vtc, p_ref[...], preferred_element_type=F32)

    def softmax_chunk(s_ref, cm_ref, p_ref):
        m_prev = m_ref[...]
        m_new = jnp.maximum(m_prev, cm_ref[...])
        p_ref[...] = jnp.exp2(s_ref[...] - m_new).astype(BF16)
        m_ref[...] = m_new
        return jnp.exp2(m_prev - m_new)

    def step(c, s_cur, cm_cur, s_next, cm_next, p_cur, p_prev):
        scores(c + 1, s_next, cm_next)
        pending = weighted_values(c - 1, p_prev)
        alpha = softmax_chunk(s_cur, cm_cur, p_cur)
        acc_ref[...] = (acc_ref[...] + pending) * alpha

    acc_ref[...] = jnp.zeros(acc_ref.shape, F32)
    p_b[...] = jnp.zeros(p_b.shape, BF16)
    load_queries(0)
    scores(0, s_a, cm_a)

    even = (s_a, cm_a, s_b, cm_b, p_a, p_b)
    odd = (s_b, cm_b, s_a, cm_a, p_b, p_a)

    def tile(i, carry):
        scores(1, s_b, cm_b)
        accf_ref[...] = acc_ref[...] + weighted_values(n_chunks - 1, p_b)
        m_ref[...] = jnp.full(m_ref.shape, -1e30, F32)
        softmax_chunk(s_a, cm_a, p_a)
        acc_ref[...] = jnp.zeros(acc_ref.shape, F32)
        step(1, *odd)

        @pl.when(i > 0)
        def _():
            finish_tile(i - 1)

        def pair(j, carry):
            c = 2 + 2 * j
            step(c, *even)
            step(c + 1, *odd)
            return carry
        lax.fori_loop(0, (n_chunks - 4) // 2, pair, 0)

        step(n_chunks - 2, *even)
        load_queries(jnp.minimum(i + 1, n_tiles - 1))
        scores(0, s_a, cm_a)
        pending = weighted_values(n_chunks - 2, p_a)
        alpha = softmax_chunk(s_b, cm_b, p_b)
        acc_ref[...] = (acc_ref[...] + pending) * alpha
        return carry
    lax.fori_loop(0, n_tiles, tile, 0)

    accf_ref[...] = acc_ref[...] + weighted_values(n_chunks - 1, p_b)
    finish_tile(n_tiles - 1)


def _diff_attention(qk, vm, gates, g_sub, lq1, lk1, lq2, lk2, lam_init, batch, seq, tq=2048, tk=512):
    head_blocks = GROUP_W // LANES
    vec = lambda a: a.reshape(1, -1).astype(F32)
    small = lambda n: pl.BlockSpec((1, n), lambda b, h: (0, 0))
    once = pl.Buffered(1)
    return pl.pallas_call(
        functools.partial(_diff_attn_kernel, lam_init=lam_init, tq=tq, tk=tk, t_chunk=512),
        grid=(batch, DA_HEADS),
        in_specs=[
            pl.BlockSpec((seq, LANES), lambda b, h: (b, h), pipeline_mode=once),
            pl.BlockSpec((seq, LANES), lambda b, h: (b, head_blocks + h)),
            pl.BlockSpec((seq, LANES), lambda b, h: (b, h)),
            pl.BlockSpec((seq, LANES), lambda b, h: (b, h), pipeline_mode=once),
            small(DA_V_DIM), small(DA_QK_DIM), small(DA_QK_DIM), small(DA_QK_DIM), small(DA_QK_DIM),
        ],
        out_specs=pl.BlockSpec((seq, LANES), lambda b, h: (b, h)),
        out_shape=jax.ShapeDtypeStruct((batch * seq, GROUP_W), BF16),
        scratch_shapes=[
            pltpu.VMEM((DA_V_DIM + BF16_SUBLANES, seq), BF16),
            pltpu.VMEM((2 * tq, LANES), BF16),
            pltpu.VMEM((1, 2 * tq), F32),
            pltpu.VMEM((DA_V_DIM + BF16_SUBLANES, 2 * tq), F32),
            pltpu.VMEM((DA_V_DIM + BF16_SUBLANES, 2 * tq), F32),
            pltpu.VMEM((tk, 2 * tq), F32), pltpu.VMEM((tk, 2 * tq), F32),
            pltpu.VMEM((1, 2 * tq), F32), pltpu.VMEM((1, 2 * tq), F32),
            pltpu.VMEM((tk, 2 * tq), BF16), pltpu.VMEM((tk, 2 * tq), BF16),
        ],
        compiler_params=pltpu.CompilerParams(dimension_semantics=("parallel", "parallel"),
                                             vmem_limit_bytes=ATTN_VMEM_LIMIT),
        name="diff_attention",
    )(qk, qk, vm, gates, vec(g_sub), vec(lq1), vec(lk1), vec(lq2), vec(lk2))


def _filter_kernel(z_ref, w1_ref, b1_ref, w2_ref, b2_ref, w3_ref, b3_ref, w4_ref, fr_ref, delta_ref, o_ref):
    hp = lax.Precision.HIGHEST
    fr = fr_ref[...]

    def dot_split(a, b):
        a_hi = a.astype(BF16)
        a_lo = (a - a_hi.astype(F32)).astype(BF16)
        b_hi = b.astype(BF16)
        b_lo = (b - b_hi.astype(F32)).astype(BF16)
        return (jnp.dot(a_hi, b_hi, preferred_element_type=F32)
                + (jnp.dot(a_hi, b_lo, preferred_element_type=F32)
                   + jnp.dot(a_lo, b_hi, preferred_element_type=F32)))

    z = z_ref[...]
    h = jnp.sin(fr * (jnp.dot(w1_ref[...], z, precision=hp, preferred_element_type=F32) + b1_ref[...]))
    h = jnp.sin(fr * (jnp.dot(w2_ref[...], h, precision=hp, preferred_element_type=F32) + b2_ref[...]))
    h = jnp.sin(fr * (jnp.dot(w3_ref[...], h, precision=hp, preferred_element_type=F32) + b3_ref[...]))
    decay = jnp.exp(-delta_ref[...] * z[0:1, :])
    o_ref[...] = (dot_split(w4_ref[...], h) * decay).astype(o_ref.dtype)


def _hyena_filters(zt, deltas, w1, b1, w2, b2, w3, b3, w4, freq, tl=512):
    seq = zt.shape[1]
    k = HF_ORDER
    n_out = w4.shape[1]
    w1t = jnp.zeros((k, k), F32).at[:, :HF_EMB].set(w1.astype(F32).T)
    colv = lambda a: a.reshape(-1, 1).astype(F32)
    full = lambda shape: pl.BlockSpec(shape, lambda i: (0, 0))
    return pl.pallas_call(
        _filter_kernel,
        grid=(seq // tl,),
        in_specs=[
            pl.BlockSpec((k, tl), lambda i: (0, i)),
            full((k, k)), full((k, 1)), full((k, k)), full((k, 1)), full((k, k)), full((k, 1)),
            full((n_out, k)), full((k, 1)), full((n_out, 1)),
        ],
        out_specs=pl.BlockSpec((n_out, tl), lambda i: (0, i)),
        out_shape=jax.ShapeDtypeStruct((n_out, seq), BF16),
        compiler_params=_cparams(("parallel",)),
        name="hyena_filters",
    )(zt, w1t, colv(b1), w2.astype(F32).T, colv(b2), w3.astype(F32).T, colv(b3), w4.astype(F32).T,
      colv(freq), deltas)


def _dft_tables():
    k1 = np.arange(FFT_R, dtype=np.int64)[:, None]
    n1 = np.arange(FFT_HALF, dtype=np.int64)[None, :]
    a1 = -2.0 * np.pi * ((k1 * n1) % FFT_R) / FFT_R
    f1 = np.concatenate([np.cos(a1), np.sin(a1)], axis=0)
    f1_inv = np.concatenate([np.cos(a1).T, np.sin(a1).T], axis=1)
    n2 = np.arange(FFT_R, dtype=np.int64)[None, :]
    at = -2.0 * np.pi * ((k1 * n2) % FFT_N) / FFT_N
    k2 = np.arange(FFT_R, dtype=np.int64)[None, :]
    a2 = -2.0 * np.pi * ((n2.T * k2) % FFT_R) / FFT_R
    fr, fi = np.cos(a2), np.sin(a2)
    c2 = np.block([[fr, fi], [-fi, fr]])
    c2_inv = np.block([[fr, -fi], [fi, fr]])
    f1r, f1i = np.cos(a1), np.sin(a1)
    f1c = np.block([[f1r, -f1i], [f1i, f1r]])
    f1c_inv = np.block([[f1r.T, f1i.T], [-f1i.T, f1r.T]])
    as_bf16 = lambda a: jnp.asarray(a, F32).astype(BF16)
    return dict(f1=as_bf16(f1), f1_inv=as_bf16(f1_inv), c2=as_bf16(c2), c2_inv=as_bf16(c2_inv),
                f1c=as_bf16(f1c), f1c_inv=as_bf16(f1c_inv),
                tr=jnp.asarray(np.cos(at), F32), ti=jnp.asarray(np.sin(at), F32))


def _dft_many(xs, f1_ref, c2_ref, tr, ti):
    r = FFT_R
    xx = jnp.concatenate([x.astype(BF16) for x in xs], axis=1)
    a = jnp.dot(f1_ref[...], xx, preferred_element_type=F32)
    rows = []
    for s in range(len(xs)):
        ar, ai = a[:r, s * r:(s + 1) * r], a[r:, s * r:(s + 1) * r]
        rows.append(jnp.concatenate([ar * tr - ai * ti, ar * ti + ai * tr], axis=1).astype(BF16))
    return jnp.dot(jnp.concatenate(rows, axis=0), c2_ref[...], preferred_element_type=F32)


def _idft_many(ys, f1i_ref, c2i_ref, tr, ti):
    r = FFT_R
    y = jnp.concatenate([v.astype(BF16) for v in ys], axis=0)
    b = jnp.dot(y, c2i_ref[...], preferred_element_type=F32)
    outs_r, outs_i = [], []
    for s in range(len(ys)):
        br, bi = b[s * r:(s + 1) * r, :r], b[s * r:(s + 1) * r, r:]
        outs_r.append((br * tr + bi * ti).astype(BF16))
        outs_i.append((bi * tr - br * ti).astype(BF16))
    rhs = jnp.concatenate([jnp.concatenate(outs_r, axis=1), jnp.concatenate(outs_i, axis=1)], axis=0)
    return jnp.dot(f1i_ref[...], rhs, preferred_element_type=F32)


def _delay_one(u):
    lane = lax.broadcasted_iota(jnp.int32, u.shape, 1)
    row = lax.broadcasted_iota(jnp.int32, u.shape, 0)
    prev = pltpu.roll(u, 1, 1)
    return jnp.where(lane == 0, jnp.where(row == 0, 0.0, pltpu.roll(prev, 1, 0)), prev)


def _filter_spectrum_kernel(hf_ref, hb_ref, f1_ref, c2_ref, tr_ref, ti_ref, o_ref):
    r = FFT_R
    cb = hf_ref.shape[2]
    xs = []
    for ci in range(cb):
        xs.append(hf_ref[0, 0, ci])
        xs.append(_delay_one(hb_ref[0, 0, ci].astype(F32)))
    x = _dft_many(xs, f1_ref, c2_ref, tr_ref[...], ti_ref[...])
    for ci in range(cb):
        xf, xb = x[2 * ci * r:(2 * ci + 1) * r], x[(2 * ci + 1) * r:(2 * ci + 2) * r]
        o_ref[0, ci] = jnp.concatenate([xf[:, :r] + xb[:, :r], xf[:, r:] - xb[:, r:]],
                                       axis=1).astype(o_ref.dtype)


def _filter_spectrum(filt5, tabs):
    n_ord, _, n_ch, rows, _ = filt5.shape
    cb = 2 * HY_CBLK
    const = lambda shape: pl.BlockSpec(shape, lambda o, c: (0, 0))
    return pl.pallas_call(
        _filter_spectrum_kernel,
        grid=(n_ord, n_ch // cb),
        in_specs=[
            pl.BlockSpec((1, 1, cb, rows, LANES), lambda o, c: (o, 0, c, 0, 0)),
            pl.BlockSpec((1, 1, cb, rows, LANES), lambda o, c: (o, 1, c, 0, 0)),
            const((2 * FFT_R, FFT_HALF)), const((2 * FFT_R, 2 * FFT_R)),
            const((FFT_R, FFT_R)), const((FFT_R, FFT_R)),
        ],
        out_specs=pl.BlockSpec((1, cb, FFT_R, 2 * FFT_R), lambda o, c: (o, c, 0, 0)),
        out_shape=jax.ShapeDtypeStruct((n_ord, n_ch, FFT_R, 2 * FFT_R), BF16),
        compiler_params=_cparams(("parallel", "parallel")),
        name="filter_spectrum",
    )(filt5, filt5, tabs["f1"], tabs["c2"], tabs["tr"], tabs["ti"])


def _short_conv_mat(u, w0, w1, w2, bias):
    rows, lanes = u.shape
    lane = lax.broadcasted_iota(jnp.int32, u.shape, 1)
    row = lax.broadcasted_iota(jnp.int32, u.shape, 0)
    prev = _delay_one(u)
    nxt = pltpu.roll(u, lanes - 1, 1)
    nxt = jnp.where(lane == lanes - 1, jnp.where(row == rows - 1, 0.0, pltpu.roll(nxt, rows - 1, 0)), nxt)
    return w0 * prev + w1 * u + w2 * nxt + bias


def _hyena_conv_kernel(cw_ref, d_ref, u_ref, gate_ref, *refs, conv_in, gated_out, u_ch0, gate_ch0, order_ch0):
    if gated_out:
        g2_ref, k_ref, f1_ref, f1i_ref, c2_ref, c2i_ref, tr_ref, ti_ref, o_ref = refs
    else:
        k_ref, f1_ref, f1i_ref, c2_ref, c2i_ref, tr_ref, ti_ref, o_ref = refs
        g2_ref = None
    r = FFT_R
    tr, ti = tr_ref[...], ti_ref[...]
    batch, cb = u_ref.shape[0], u_ref.shape[1]
    c0 = pl.program_id(0) * cb
    signals = [(ci, b) for ci in range(cb) for b in range(batch)]

    def conv_taps(ref, b, ci, ch):
        return _short_conv_mat(ref[b, ci], cw_ref[0, ch], cw_ref[1, ch], cw_ref[2, ch], cw_ref[3, ch])

    def long_conv_input(ci, b):
        return conv_taps(u_ref, b, ci, u_ch0 + c0 + ci) if conv_in else u_ref[b, ci]

    assert batch == 2
    half = FFT_HALF
    us = [long_conv_input(ci, b) for ci, b in signals]
    packed = [jnp.concatenate([us[2 * ci], us[2 * ci + 1]], axis=0) for ci in range(cb)]
    x = _dft_many(packed, f1_ref, c2_ref, tr, ti)
    ys = []
    for ci in range(cb):
        kk = k_ref[0, ci].astype(F32)
        kr, ki = kk[:, :r], kk[:, r:]
        xr, xi = x[ci * r:(ci + 1) * r, :r], x[ci * r:(ci + 1) * r, r:]
        ys.append(jnp.concatenate([xr * kr - xi * ki, xr * ki + xi * kr], axis=1))
    y = _idft_many(ys, f1i_ref, c2i_ref, tr, ti)
    for s, (ci, b) in enumerate(signals):
        gate = conv_taps(gate_ref, b, ci, gate_ch0 + c0 + ci)
        skip = d_ref[0, order_ch0 + c0 + ci] * us[s]
        res = gate * (y[b * half:(b + 1) * half, ci * r:(ci + 1) * r] * (1.0 / FFT_N) + skip)
        if gated_out:
            res = res * _silu(g2_ref[b, ci])
        o_ref[b, ci] = res.astype(o_ref.dtype)


def _hyena_conv(conv_wb, d_skip, u_arr, u_ch0, gate_arr, gate_ch0, g2, spec, order, tabs, out_dtype, conv_in):
    batch = u_arr.shape[0]
    assert batch == 2 and u_arr.shape[2] == FFT_HALF
    cb = HY_CBLK
    smem = pl.BlockSpec(memory_space=pltpu.SMEM)
    chan = lambda ch0: pl.BlockSpec((batch, cb, FFT_HALF, LANES), lambda c: (0, ch0 // cb + c, 0, 0))
    const = lambda shape: pl.BlockSpec(shape, lambda c: (0, 0))
    in_specs = [smem, smem, chan(u_ch0), chan(gate_ch0)]
    args = [conv_wb, d_skip, u_arr, gate_arr]
    if g2 is not None:
        in_specs.append(chan(g2[1]))
        args.append(g2[0])
    in_specs += [
        pl.BlockSpec((1, cb, FFT_R, 2 * FFT_R), lambda c: (order, c, 0, 0)),
        const((2 * FFT_R, FFT_R)), const((FFT_R, 2 * FFT_R)),
        const((2 * FFT_R, 2 * FFT_R)), const((2 * FFT_R, 2 * FFT_R)),
        const((FFT_R, FFT_R)), const((FFT_R, FFT_R)),
    ]
    args += [spec, tabs["f1c"], tabs["f1c_inv"], tabs["c2"], tabs["c2_inv"], tabs["tr"], tabs["ti"]]
    return pl.pallas_call(
        functools.partial(_hyena_conv_kernel, conv_in=conv_in, gated_out=g2 is not None,
                          u_ch0=u_ch0, gate_ch0=gate_ch0, order_ch0=order * HY_CH),
        grid=(HY_CH // cb,),
        in_specs=in_specs,
        out_specs=pl.BlockSpec((batch, cb, FFT_HALF, LANES), lambda c: (0, c, 0, 0)),
        out_shape=jax.ShapeDtypeStruct((batch, HY_CH, FFT_HALF, LANES), out_dtype),
        compiler_params=_cparams(("parallel",)),
        name="hyena_conv_gated" if g2 is not None else "hyena_conv",
    )(*args)


def _out_kernel(ya_ref, ybt_ref, mq_ref, mg_ref, mk_ref, mv_ref, w_ref, x_ref, gf_ref, o_ref, *, final):
    gw = GROUP_W
    acc = x_ref[...]
    acc += jnp.dot(ya_ref[...], w_ref[0:gw, :], preferred_element_type=F32)
    acc += lax.dot_general(ybt_ref[0], w_ref[gw:2 * gw, :], (((0,), (0,)), ((), ())),
                           preferred_element_type=F32)
    hd = MEM_HEAD_DIM
    heads = []
    for h in range(MEM_HEADS):
        sl = slice(h * hd, (h + 1) * hd)
        s = lax.dot_general(mq_ref[:, sl], mk_ref[:, sl], (((1,), (1,)), ((), ())),
                            preferred_element_type=F32) * (hd ** -0.5)
        e = jnp.exp(s - jnp.max(s, axis=-1, keepdims=True))
        p = e / jnp.sum(e, axis=-1, keepdims=True)
        oh = jnp.dot(p.astype(BF16), mv_ref[:, sl], preferred_element_type=F32)
        heads.append((oh * _silu(mg_ref[:, sl])).astype(BF16))
    acc += jnp.dot(jnp.concatenate(heads, axis=1), w_ref[2 * gw:3 * gw, :], preferred_element_type=F32)
    if final:
        acc = _rms_norm_rows(acc, gf_ref[...])
    o_ref[...] = acc


def _out_proj(ya, ybt, vm, feats, mg_blk, mkv, w_out, x, g_final, final, seq, mem_len, tm=1024):
    m, d = x.shape
    gw = GROUP_W
    per_batch = seq // tm
    return pl.pallas_call(
        functools.partial(_out_kernel, final=final),
        grid=(m // tm,),
        in_specs=[
            pl.BlockSpec((tm, gw), lambda i: (i, 0)),
            pl.BlockSpec((1, gw, tm), lambda i: (i // per_batch, 0, i % per_batch)),
            pl.BlockSpec((tm, gw), lambda i: (i, 1)),
            pl.BlockSpec((tm, gw), lambda i: (i, mg_blk)),
            pl.BlockSpec((mem_len, gw), lambda i: (i // per_batch, 0)),
            pl.BlockSpec((mem_len, gw), lambda i: (i // per_batch, 1)),
            pl.BlockSpec((3 * gw, d), lambda i: (0, 0)),
            pl.BlockSpec((tm, d), lambda i: (i, 0)),
            pl.BlockSpec((1, d), lambda i: (0, 0)),
        ],
        out_specs=pl.BlockSpec((tm, d), lambda i: (i, 0)),
        out_shape=jax.ShapeDtypeStruct((m, d), F32),
        compiler_params=_cparams(("parallel",)),
        name="out_proj_final" if final else "out_proj",
    )(ya, ybt, vm, feats, mkv, mkv, w_out, x, g_final.reshape(1, d).astype(F32))


def _rope_tables(seq):
    pos = np.arange(seq, dtype=np.float64)
    inv_freq = ROPE_THETA ** (-np.arange(0, DA_QK_DIM, 2, dtype=np.float64) / DA_QK_DIM)
    ang = pos[:, None] * inv_freq[None, :]
    cos, sin = np.cos(ang), np.sin(ang)
    reps = LANES // DA_QK_DIM
    return (jnp.asarray(np.tile(np.concatenate([cos, cos], axis=-1), (1, reps)), F32),
            jnp.asarray(np.tile(np.concatenate([-sin, sin], axis=-1), (1, reps)), F32))


def _filter_features(seq):
    t = np.linspace(0.0, 1.0, seq)[:, None]
    w = 2.0 * math.pi * np.arange(seq, dtype=np.float64)[:, None] / seq
    f = np.linspace(1e-4, HF_BANDS - 1, HF_BANDS)[None, :]
    z = np.concatenate([t, np.cos(f * w), -np.sin(f * w)], axis=-1)
    z = np.pad(z, ((0, 0), (0, HF_ORDER - HF_EMB)))
    max_decay = math.log(HF_TARGET) / HF_FAST
    min_decay = math.log(HF_TARGET) / HF_SLOW
    deltas = np.abs(np.linspace(min_decay, max_decay, HY_CH))
    per_order = np.concatenate([deltas, deltas[::-1]])
    as_f32 = lambda a: jnp.asarray(np.ascontiguousarray(a), F32)
    return as_f32(z.T), as_f32(np.tile(per_order, 2)[:, None])


def kernel(x, mem, g_norm, w_in, da_lam_q1, da_lam_k1, da_lam_q2, da_lam_k2, da_subln_g, hy_conv_w, hy_conv_b, hf_w1, hf_b1, hf_w2, hf_b2, hf_w3, hf_b3, hf_w4, hf_freq, hy_skip, g_mem, w_mem_kv, w_out, g_final):
    batch, seq, d = x.shape
    assert seq == SEQ_LEN
    mem_len = mem.shape[1]
    depth = w_in.shape[0]
    gw = GROUP_W
    xf = x.reshape(batch * seq, d).astype(F32)
    memf = mem.reshape(batch * mem_len, d).astype(F32)
    rope_tabs = _rope_tables(seq)
    zt, deltas = _filter_features(seq)
    tabs = _dft_tables()

    for l in range(depth):
        w = w_in[l].astype(BF16)
        w_hy_t = w[:, 4 * gw:8 * gw].T
        qk, vm, gates, hy = _input_proj(xf, g_norm[l], w, w_hy_t, rope_tabs,
                                        DA_QK_DIM ** -0.5 * math.log2(math.e),
                                        batch, seq)
        hy = hy.reshape(batch, 4 * HY_CH, FFT_HALF, LANES)
        mkv = _norm_matmul(memf, g_mem[l], w_mem_kv[l].astype(BF16), BF16, "proj_mem_kv")

        lam_init = 0.8 - 0.6 * math.exp(-0.3 * l)
        ya = _diff_attention(qk, vm, gates, da_subln_g[l], da_lam_q1[l], da_lam_k1[l],
                             da_lam_q2[l], da_lam_k2[l], lam_init, batch, seq)

        filt = _hyena_filters(zt, deltas, hf_w1[l], hf_b1[l], hf_w2[l], hf_b2[l],
                              hf_w3[l], hf_b3[l], hf_w4[l], hf_freq[l])
        spec = _filter_spectrum(filt.reshape(2, 2, HY_CH, FFT_HALF, LANES), tabs)
        conv_wb = jnp.concatenate([hy_conv_w[l], hy_conv_b[l][None, :]], axis=0).astype(F32)
        d_skip = hy_skip[l].reshape(1, -1).astype(F32)
        z1 = _hyena_conv(conv_wb, d_skip, hy, 0, hy, HY_CH, None, spec, 0, tabs, F32, conv_in=True)
        ybt = _hyena_conv(conv_wb, d_skip, z1, 0, hy, 2 * HY_CH, (hy, 3 * HY_CH), spec, 1, tabs, BF16,
                          conv_in=False)
        ybt = ybt.reshape(batch, HY_CH, seq)

        xf = _out_proj(ya, ybt, vm, gates, 1, mkv, w_out[l].astype(BF16), xf, g_final,
                       l == depth - 1, seq, mem_len)
    return xf.reshape(batch, seq, d)
```
